```python
import math
import jax
import jax.numpy as jnp
from jax import lax
import numpy as np

D_MODEL = 1024
BATCH = 16
SEQ = 4096
DEPTH = 2

GRID_W = 64
CTX_LEN = 256
HD = 64
D_A = D_MODEL // 2
D_B = D_MODEL - D_A
H_A = D_A // (2 * HD)
H_B = D_B // HD
QBLK = 128
WIN_R = 8
WIN_C = 16
ROPE_THETA = 10000.0
D_RNN = D_MODEL
N_RG_BLOCKS = 8
RG_BW = D_RNN // N_RG_BLOCKS
RG_C = 8.0
CONV_RNN = 4
D_FF = ((8 * D_MODEL // 3 + 127) // 128) * 128
CONV_FFN = 3
N_EVEN = (DEPTH + 1) // 2
N_ODD = DEPTH // 2
DEEPNORM_ALPHA = (2 * DEPTH) ** 0.25
DEEPNORM_BETA = (8 * DEPTH) ** -0.25
NORM_EPS = 1e-5

kernel_name = 'hybrid_diffattn_natten_rglru_dit'


def layer_norm(x, g, b):
    xf = x.astype(jnp.float32)
    mu = jnp.mean(xf, -1, keepdims=True)
    var = jnp.mean(jnp.square(xf - mu), -1, keepdims=True)
    return ((xf - mu) * lax.rsqrt(var + NORM_EPS)).astype(x.dtype) * g + b


def head_rms_norm(o, g):
    of = o.astype(jnp.float32)
    of = of * lax.rsqrt(jnp.mean(jnp.square(of), -1, keepdims=True) + NORM_EPS)
    return of.astype(o.dtype) * g


def dwconv_centred(x, w, b):
    k = w.shape[0]
    left = k // 2
    out = lax.conv_general_dilated(
        x, w[:, None, :].astype(x.dtype), window_strides=(1,),
        padding=[(left, k - 1 - left)], dimension_numbers=('NWC', 'WIO', 'NWC'),
        feature_group_count=x.shape[-1])
    return out + b


def axial_rope(n_tok):
    t = jnp.arange(n_tok)
    row = (t // GRID_W).astype(jnp.float32)[:, None]
    col = (t % GRID_W).astype(jnp.float32)[:, None]
    nf = HD // 4
    inv = 1.0 / (ROPE_THETA ** (jnp.arange(nf, dtype=jnp.float32) / nf))
    ang = jnp.concatenate([row * inv, row * inv, col * inv, col * inv], -1)
    return jnp.cos(ang), jnp.sin(ang)


def apply_rope(x, cos, sin):
    xr = x.reshape(x.shape[:-1] + (2, 2, HD // 4))
    rot = jnp.stack([-xr[..., 1, :], xr[..., 0, :]], -2).reshape(x.shape)
    cos = cos[None, :, None, None, :].astype(x.dtype)
    sin = sin[None, :, None, None, :].astype(x.dtype)
    return x * cos + rot * sin


def blockwise(fn, q):
    b, t = q.shape[:2]
    nb = t // QBLK
    qb = jnp.moveaxis(q.reshape((b, nb, QBLK) + q.shape[2:]), 1, 0)
    out = lax.map(fn, qb)
    return jnp.moveaxis(out, 0, 1).reshape((b, t) + out.shape[3:])


def diff_attend(q, k, v, lam):
    s = jnp.einsum('bqhmd,bkhmd->bhmqk', q, k).astype(jnp.float32) * (HD ** -0.5)
    p = jax.nn.softmax(s, axis=-1)
    w = (p[:, :, 0] - lam * p[:, :, 1]).astype(v.dtype)
    return jnp.einsum('bhqk,bkhe->bqhe', w, v)


def softmax_attend(q, k, v):
    s = jnp.einsum('bqhd,bkhd->bhqk', q, k).astype(jnp.float32) * (q.shape[-1] ** -0.5)
    p = jax.nn.softmax(s, axis=-1).astype(v.dtype)
    return jnp.einsum('bhqk,bkhd->bqhd', p, v)


def neighbourhood_attend(q, k, v, k_ctx, v_ctx, rpb):
    b, t, h, d = q.shape
    rows = t // GRID_W
    wr = min(WIN_R, rows)
    kg = k.reshape(b, rows, GRID_W, h, d)
    vg = v.reshape(b, rows, GRID_W, h, d)
    qg = jnp.moveaxis(q.reshape(b, rows, GRID_W, h, d), 1, 0)
    cols = jnp.arange(GRID_W)
    col_idx = jnp.clip(cols - WIN_C // 2, 0, GRID_W - WIN_C)[:, None] + jnp.arange(WIN_C)[None]
    rel_c = col_idx - cols[:, None] + (WIN_C - 1)
    scale = d ** -0.5
    n_loc = wr * WIN_C

    def row_fn(args):
        r, q_row = args
        rs = jnp.clip(r - wr // 2, 0, rows - wr)
        kb = lax.dynamic_slice_in_dim(kg, rs, wr, axis=1)[:, :, col_idx]
        vb = lax.dynamic_slice_in_dim(vg, rs, wr, axis=1)[:, :, col_idx]
        rel_r = rs + jnp.arange(wr) - r + (WIN_R - 1)
        bias = jnp.transpose(rpb[:, rel_r[:, None, None], rel_c[None]], (0, 2, 1, 3))
        s_loc = jnp.einsum('bqhd,brqchd->bhqrc', q_row, kb).astype(jnp.float32) * scale
        s_loc = s_loc + bias.astype(jnp.float32)[None]
        s_ctx = jnp.einsum('bqhd,bkhd->bhqk', q_row, k_ctx).astype(jnp.float32) * scale
        s = jnp.concatenate([s_loc.reshape(b, h, GRID_W, n_loc), s_ctx], -1)
        p = jax.nn.softmax(s, axis=-1).astype(v.dtype)
        p_loc = p[..., :n_loc].reshape(b, h, GRID_W, wr, WIN_C)
        return (jnp.einsum('bhqrc,brqchd->bqhd', p_loc, vb)
                + jnp.einsum('bhqk,bkhd->bqhd', p[..., n_loc:], v_ctx))

    out = lax.map(row_fn, (jnp.arange(rows), qg))
    return jnp.moveaxis(out, 0, 1).reshape(b, t, h, d)


def even_mixer(h_lat, h_ctx, w_in, w_out, lq1, lk1, lq2, lk2, subln_g, rpb, lambda_init, need_ctx):
    b, t, _ = h_lat.shape
    n_ctx = h_ctx.shape[1]
    cos, sin = axial_rope(t)
    lam = (jnp.exp(jnp.sum(lq1 * lk1).astype(jnp.float32))
           - jnp.exp(jnp.sum(lq2 * lk2).astype(jnp.float32)) + lambda_init)

    def split(p, n):
        qa, ka, va, qb, kb, vb = jnp.split(
            p, [D_A, 2 * D_A, 3 * D_A, 3 * D_A + D_B, 3 * D_A + 2 * D_B], axis=-1)
        return (qa.reshape(b, n, H_A, 2, HD), ka.reshape(b, n, H_A, 2, HD),
                va.reshape(b, n, H_A, 2 * HD), qb.reshape(b, n, H_B, HD),
                kb.reshape(b, n, H_B, HD), vb.reshape(b, n, H_B, HD))

    qa, ka, va, qb, kb, vb = split(h_lat @ w_in, t)
    qa_c, ka_c, va_c, qb_c, kb_c, vb_c = split(h_ctx @ w_in, n_ctx)
    qa = apply_rope(qa, cos, sin)
    ka = apply_rope(ka, cos, sin)
    k_all = jnp.concatenate([ka_c, ka], axis=1)
    v_all = jnp.concatenate([va_c, va], axis=1)
    oa = blockwise(lambda qblk: diff_attend(qblk, k_all, v_all, lam), qa)
    oa = head_rms_norm(oa, subln_g) * (1.0 - lambda_init)
    ob = neighbourhood_attend(qb, kb, vb, kb_c, vb_c, rpb)
    out_lat = jnp.concatenate([oa.reshape(b, t, D_A), ob.reshape(b, t, D_B)], -1) @ w_out
    if not need_ctx:
        return out_lat, None
    oa_c = head_rms_norm(diff_attend(qa_c, ka_c, va_c, lam), subln_g) * (1.0 - lambda_init)
    ob_c = softmax_attend(qb_c, kb_c, vb_c)
    out_ctx = jnp.concatenate([oa_c.reshape(b, n_ctx, D_A), ob_c.reshape(b, n_ctx, D_B)], -1) @ w_out
    return out_lat, out_ctx


def rg_lru_coeffs(x, a_param, wa, ba, wx, bx):
    b, t, _ = x.shape
    xb = x.reshape(b, t, N_RG_BLOCKS, RG_BW).astype(jnp.float32)
    gate_x = jax.nn.sigmoid(jnp.einsum('btni,nij->btnj', xb, wx.astype(jnp.float32)) + bx)
    gate_a = jax.nn.sigmoid(jnp.einsum('btni,nij->btnj', xb, wa.astype(jnp.float32)) + ba)
    log_a = -RG_C * gate_a * jax.nn.softplus(-a_param.astype(jnp.float32)).reshape(N_RG_BLOCKS, RG_BW)
    a = jnp.exp(log_a)
    bterm = gate_x * xb * jnp.sqrt(-jnp.expm1(2.0 * log_a))
    return a.reshape(b, t, D_RNN), bterm.reshape(b, t, D_RNN)


def linear_scan(a, bterm, h0, reverse):
    def combine(e1, e2):
        a1, b1 = e1
        a2, b2 = e2
        return a1 * a2, a2 * b1 + b2
    a_cum, b_cum = lax.associative_scan(combine, (a, bterm), reverse=reverse, axis=1)
    if h0 is None:
        return b_cum
    return a_cum * h0[:, None, :] + b_cum


def odd_mixer(h_lat, h_ctx, w_in, conv_w, conv_b, a_param, wa, ba, wx, bx, w_out, need_ctx):
    y_l, x_l = jnp.split(h_lat @ w_in, 2, axis=-1)
    y_c, x_c = jnp.split(h_ctx @ w_in, 2, axis=-1)
    x_l = dwconv_centred(x_l, conv_w, conv_b)
    x_c = dwconv_centred(x_c, conv_w, conv_b)
    r_lat = []
    r_ctx = []
    for d, reverse in enumerate((False, True)):
        a_c, b_c = rg_lru_coeffs(x_c, a_param[d], wa[d], ba[d], wx[d], bx[d])
        h_c = linear_scan(a_c, b_c, None, reverse)
        h_final = h_c[:, 0] if reverse else h_c[:, -1]
        a_l, b_l = rg_lru_coeffs(x_l, a_param[d], wa[d], ba[d], wx[d], bx[d])
        r_lat.append(linear_scan(a_l, b_l, h_final, reverse))
        r_ctx.append(h_c)
    out_lat = ((r_lat[0] + r_lat[1]).astype(h_lat.dtype) * jax.nn.gelu(y_l, approximate=True)) @ w_out
    if not need_ctx:
        return out_lat, None
    out_ctx = ((r_ctx[0] + r_ctx[1]).astype(h_ctx.dtype) * jax.nn.gelu(y_c, approximate=True)) @ w_out
    return out_lat, out_ctx


def conv_ffn(h, w_up, conv_w, conv_b, w_down):
    g, v = jnp.split(dwconv_centred(h @ w_up, conv_w, conv_b), 2, axis=-1)
    return (jax.nn.gelu(g, approximate=True) * v) @ w_down


def setup_inputs(seed: int = 0) -> dict:
    key = jax.random.key(seed)
    keys = list(jax.random.split(key, 40))
    f32 = jnp.float32

    def nrm(shape, scale):
        return jax.random.normal(keys.pop(), shape, f32) * scale

    u = jax.random.uniform(keys.pop(), (N_ODD, 2, D_RNN), f32, 0.81, 0.998)
    s = u ** (1.0 / RG_C)
    beta = DEEPNORM_BETA
    return {
        'x': nrm((BATCH, SEQ, D_MODEL), 1.0),
        'c': nrm((BATCH, D_MODEL), 1.0),
        'ctx': nrm((BATCH, CTX_LEN, D_MODEL), 1.0),
        'c_ctx': nrm((D_MODEL,), 1.0),
        'ada_w': nrm((DEPTH, D_MODEL, 6 * D_MODEL), 0.5 * D_MODEL ** -0.5),
        'ada_b': nrm((DEPTH, 6 * D_MODEL), 0.01),
        'ln1_g': 1.0 + nrm((DEPTH, D_MODEL), 0.02),
        'ln1_b': nrm((DEPTH, D_MODEL), 0.02),
        'ln2_g': 1.0 + nrm((DEPTH, D_MODEL), 0.02),
        'ln2_b': nrm((DEPTH, D_MODEL), 0.02),
        'ffn_w_up': nrm((DEPTH, D_MODEL, 2 * D_FF), D_MODEL ** -0.5),
        'ffn_conv_w': nrm((DEPTH, CONV_FFN, 2 * D_FF), CONV_FFN ** -0.5),
        'ffn_conv_b': nrm((DEPTH, 2 * D_FF), 0.01),
        'ffn_w_down': nrm((DEPTH, D_FF, D_MODEL), beta * D_FF ** -0.5),
        'att_w_in': nrm((N_EVEN, D_MODEL, 3 * (D_A + D_B)), D_MODEL ** -0.5),
        'att_w_out': nrm((N_EVEN, D_A + D_B, D_MODEL), beta * (D_A + D_B) ** -0.5),
        'diff_lq1': nrm((N_EVEN, HD), 0.1),
        'diff_lk1': nrm((N_EVEN, HD), 0.1),
        'diff_lq2': nrm((N_EVEN, HD), 0.1),
        'diff_lk2': nrm((N_EVEN, HD), 0.1),
        'diff_subln_g': 1.0 + nrm((N_EVEN, 2 * HD), 0.02),
        'na_rpb': nrm((N_EVEN, H_B, 2 * WIN_R - 1, 2 * WIN_C - 1), 0.02),
        'rnn_w_in': nrm((N_ODD, D_MODEL, 2 * D_RNN), D_MODEL ** -0.5),
        'rnn_conv_w': nrm((N_ODD, CONV_RNN, D_RNN), CONV_RNN ** -0.5),
        'rnn_conv_b': nrm((N_ODD, D_RNN), 0.01),
        'rg_a_param': jnp.log(s) - jnp.log1p(-s),
        'rg_wa': nrm((N_ODD, 2, N_RG_BLOCKS, RG_BW, RG_BW), RG_BW ** -0.5),
        'rg_ba': nrm((N_ODD, 2, N_RG_BLOCKS, RG_BW), 0.01),
        'rg_wx': nrm((N_ODD, 2, N_RG_BLOCKS, RG_BW, RG_BW), RG_BW ** -0.5),
        'rg_bx': nrm((N_ODD, 2, N_RG_BLOCKS, RG_BW), 0.01),
        'rnn_w_out': nrm((N_ODD, D_RNN, D_MODEL), beta * D_RNN ** -0.5),
    }


def reference(x, c, ctx, c_ctx, ada_w, ada_b, ln1_g, ln1_b, ln2_g, ln2_b,
              ffn_w_up, ffn_conv_w, ffn_conv_b, ffn_w_down,
              att_w_in, att_w_out, diff_lq1, diff_lk1, diff_lq2, diff_lk2, diff_subln_g, na_rpb,
              rnn_w_in, rnn_conv_w, rnn_conv_b, rg_a_param, rg_wa, rg_ba, rg_wx, rg_bx, rnn_w_out):
    alpha = DEEPNORM_ALPHA
    for l in range(DEPTH):
        last = l == DEPTH - 1
        mod_lat = (jax.nn.silu(c) @ ada_w[l] + ada_b[l])[:, None, :]
        mod_ctx = jax.nn.silu(c_ctx) @ ada_w[l] + ada_b[l]
        sh1, sc1, g1, sh2, sc2, g2 = jnp.split(mod_lat, 6, axis=-1)
        csh1, csc1, cg1, csh2, csc2, cg2 = jnp.split(mod_ctx, 6, axis=-1)
        h_lat = x * (1.0 + sc1) + sh1
        h_ctx = ctx * (1.0 + csc1) + csh1
        i = l // 2
        if l % 2 == 0:
            lambda_init = 0.8 - 0.6 * math.exp(-0.3 * l)
            o_lat, o_ctx = even_mixer(h_lat, h_ctx, att_w_in[i], att_w_out[i], diff_lq1[i], diff_lk1[i],
                                      diff_lq2[i], diff_lk2[i], diff_subln_g[i], na_rpb[i],
                                      lambda_init, not last)
        else:
            o_lat, o_ctx = odd_mixer(h_lat, h_ctx, rnn_w_in[i], rnn_conv_w[i], rnn_conv_b[i],
                                     rg_a_param[i], rg_wa[i], rg_ba[i], rg_wx[i], rg_bx[i],
                                     rnn_w_out[i], not last)
        x = layer_norm(alpha * x + g1 * o_lat, ln1_g[l], ln1_b[l])
        f_lat = conv_ffn(x * (1.0 + sc2) + sh2, ffn_w_up[l], ffn_conv_w[l], ffn_conv_b[l], ffn_w_down[l])
        x = layer_norm(alpha * x + g2 * f_lat, ln2_g[l], ln2_b[l])
        if not last:
            ctx = layer_norm(alpha * ctx + cg1 * o_ctx, ln1_g[l], ln1_b[l])
            f_ctx = conv_ffn(ctx * (1.0 + csc2) + csh2, ffn_w_up[l], ffn_conv_w[l], ffn_conv_b[l], ffn_w_down[l])
            ctx = layer_norm(alpha * ctx + cg2 * f_ctx, ln2_g[l], ln2_b[l])
    return x
```

```python
import functools
import math

import jax
import jax.numpy as jnp
from jax import lax
from jax.experimental import pallas as pl
from jax.experimental.pallas import tpu as pltpu

F32 = jnp.float32
BF16 = jnp.bfloat16

D_MODEL = 1024
DEPTH = 2
GRID_W = 64
HD = 64
D_A = D_MODEL // 2
D_B = D_MODEL - D_A
H_A = D_A // (2 * HD)
H_B = D_B // HD
WIN_R = 8
WIN_C = 16
ROPE_THETA = 10000.0
N_RG_BLOCKS = 8
RG_BW = D_MODEL // N_RG_BLOCKS
RG_C = 8.0
CONV_RNN = 4
D_FF = ((8 * D_MODEL // 3 + 127) // 128) * 128
CONV_FFN = 3
ALPHA = (2 * DEPTH) ** 0.25
NORM_EPS = 1e-5
NEG_BIG = -1e30

LANES = 128
HALO = 16
FF_CHUNK = 256
N_FF_CHUNKS = D_FF // FF_CHUNK
VMEM_LIMIT = 56 * 1024 * 1024

_NT = (((1,), (1,)), ((), ()))


def _cparams(n_axes):
    return pltpu.CompilerParams(dimension_semantics=("arbitrary",) * n_axes,
                                vmem_limit_bytes=VMEM_LIMIT)


def _layer_norm(y, g, b):
    mu = jnp.mean(y, axis=-1, keepdims=True)
    yc = y - mu
    var = jnp.mean(yc * yc, axis=-1, keepdims=True)
    return yc * lax.rsqrt(var + NORM_EPS) * g + b


def _softmax_parts(s_list):
    m = s_list[0].max(axis=-1, keepdims=True)
    for s in s_list[1:]:
        m = jnp.maximum(m, s.max(axis=-1, keepdims=True))
    ps = [jnp.exp(s - m) for s in s_list]
    l = ps[0].sum(axis=-1, keepdims=True)
    for p in ps[1:]:
        l = l + p.sum(axis=-1, keepdims=True)
    return ps, 1.0 / l


def _half_masks(dtype):
    lane = lax.broadcasted_iota(jnp.int32, (1, LANES), 1)
    lo = (lane < HD).astype(dtype)
    return lo, (1 - lo).astype(dtype)


def _ada_kernel(c_ref, w_ref, b_ref, o_ref):
    c = c_ref[...]
    a = c * jax.nn.sigmoid(c)
    o_ref[0] = jnp.dot(a, w_ref[0], preferred_element_type=F32,
                       precision=lax.Precision.HIGHEST) + b_ref[0]


def _ada_mod(cc, ada_w, ada_b):
    n = cc.shape[0]
    tn = 1536
    return pl.pallas_call(
        _ada_kernel,
        out_shape=jax.ShapeDtypeStruct((DEPTH, n, 6 * D_MODEL), F32),
        grid=(DEPTH, 6 * D_MODEL // tn),
        in_specs=[pl.BlockSpec((n, D_MODEL), lambda l, j: (0, 0)),
                  pl.BlockSpec((1, D_MODEL, tn), lambda l, j: (l, 0, j)),
                  pl.BlockSpec((1, 1, tn), lambda l, j: (l, 0, j))],
        out_specs=pl.BlockSpec((1, n, tn), lambda l, j: (l, 0, j)),
        compiler_params=_cparams(2), name="ada_mod",
    )(cc, ada_w, ada_b.reshape(DEPTH, 1, 6 * D_MODEL))


def _proj_attn_kernel(x_ref, sc_ref, sh_ref, w_ref, cos_ref, sin_ref,
                      qa_ref, ka_ref, va_ref, qb_ref, kb_ref, vb_ref, *, rope):
    h = (x_ref[0] * (1.0 + sc_ref[0]) + sh_ref[0]).astype(BF16)
    outs = (qa_ref, ka_ref, va_ref, qb_ref, kb_ref, vb_ref)
    for j, o_ref in enumerate(outs):
        r = jnp.dot(h, w_ref[:, j * D_A:(j + 1) * D_A], preferred_element_type=F32)
        if rope and j < 2:
            lane = lax.broadcasted_iota(jnp.int32, r.shape, 1)
            first_half = (lane % (HD // 2)) < (HD // 4)
            rot = jnp.where(first_half, pltpu.roll(r, D_A - HD // 4, 1), pltpu.roll(r, HD // 4, 1))
            r = r * cos_ref[...] + rot * sin_ref[...]
        if j in (0, 3):
            r = r * (HD ** -0.5)
        o_ref[0] = r.astype(BF16)


def _proj_attn(x, sc, sh, w, cos_t, sin_t, rope, tm):
    b, t, _ = x.shape
    out = jax.ShapeDtypeStruct((b, t, D_A), BF16)
    vec = pl.BlockSpec((1, 1, D_MODEL), lambda bi, i: (bi, 0, 0))
    blk = pl.BlockSpec((1, tm, D_A), lambda bi, i: (bi, i, 0))
    tab = pl.BlockSpec((tm, D_A), lambda bi, i: (i, 0))
    return pl.pallas_call(
        functools.partial(_proj_attn_kernel, rope=rope),
        out_shape=(out,) * 6,
        grid=(b, t // tm),
        in_specs=[pl.BlockSpec((1, tm, D_MODEL), lambda bi, i: (bi, i, 0)), vec, vec,
                  pl.BlockSpec((D_MODEL, 6 * D_A), lambda bi, i: (0, 0)), tab, tab],
        out_specs=(blk,) * 6,
        compiler_params=_cparams(2), name="proj_attn",
    )(x, sc, sh, w, cos_t, sin_t)


def _lambda_full(lp_ref, lambda_init):
    lp = lp_ref[...]
    s1 = jnp.sum(lp[0:1] * lp[1:2], axis=-1, keepdims=True)
    s2 = jnp.sum(lp[2:3] * lp[3:4], axis=-1, keepdims=True)
    return jnp.exp(s1) - jnp.exp(s2) + lambda_init


def _head_rms(o, g, lambda_init):
    o = o * lax.rsqrt(jnp.mean(o * o, axis=-1, keepdims=True) + NORM_EPS)
    return o * g * (1.0 - lambda_init)


def _diff_attn_kernel(lp_ref, g_ref, q_ref, kc_ref, vc_ref, kl_ref, vl_ref, o_ref,
                      *, tk, n_chunks, lambda_init):
    lam = _lambda_full(lp_ref, lambda_init)
    q = q_ref[0]
    lo, hi = _half_masks(BF16)
    qs = (q * lo, q * hi)
    tq = q.shape[0]

    def update(k, v, carry):
        new = []
        for mi in range(2):
            m, l, acc = carry[mi]
            s = lax.dot_general(qs[mi], k, _NT, preferred_element_type=F32)
            m_new = jnp.maximum(m, s.max(axis=-1, keepdims=True))
            a = jnp.exp(m - m_new)
            p = jnp.exp(s - m_new)
            l = a * l + p.sum(axis=-1, keepdims=True)
            acc = a * acc + jnp.dot(p.astype(BF16), v, preferred_element_type=F32)
            new.append((m_new, l, acc))
        return tuple(new)

    init = tuple((jnp.full((tq, 1), NEG_BIG, F32), jnp.zeros((tq, 1), F32),
                  jnp.zeros((tq, LANES), F32)) for _ in range(2))
    carry = update(kc_ref[0], vc_ref[0], init)

    def body(c, carry):
        off = pl.multiple_of(c * tk, tk)
        return update(kl_ref[0, pl.ds(off, tk), :], vl_ref[0, pl.ds(off, tk), :], carry)

    carry = lax.fori_loop(0, n_chunks, body, carry)
    (_, l0, a0), (_, l1, a1) = carry
    o = a0 / l0 - lam * (a1 / l1)
    o_ref[0] = _head_rms(o, g_ref[...], lambda_init).astype(BF16)


def _diff_attn(lp, g, qa, ka_c, va_c, ka, va, lambda_init, tq, tk):
    b, t, _ = qa.shape
    n_ctx = ka_c.shape[1]
    ctx_blk = pl.BlockSpec((1, n_ctx, LANES), lambda bi, h, i: (bi, 0, h))
    lat_blk = pl.BlockSpec((1, t, LANES), lambda bi, h, i: (bi, 0, h))
    q_blk = pl.BlockSpec((1, tq, LANES), lambda bi, h, i: (bi, i, h))
    return pl.pallas_call(
        functools.partial(_diff_attn_kernel, tk=tk, n_chunks=t // tk, lambda_init=lambda_init),
        out_shape=jax.ShapeDtypeStruct((b, t, D_A), BF16),
        grid=(b, H_A, t // tq),
        in_specs=[pl.BlockSpec((4, HD), lambda bi, h, i: (0, 0)),
                  pl.BlockSpec((1, 2 * HD), lambda bi, h, i: (0, 0)),
                  q_blk, ctx_blk, ctx_blk, lat_blk, lat_blk],
        out_specs=q_blk,
        compiler_params=_cparams(3), name="diff_attn",
    )(lp, g, qa, ka_c, va_c, ka, va)


def _na_kernel(q_ref, k_ref, v_ref, kc_ref, vc_ref, tab_ref, o_ref, *, rows):
    r = pl.program_id(1)
    rs = jnp.clip(r - WIN_R // 2, 0, rows - WIN_R)
    start = pl.multiple_of(rs * GRID_W, GRID_W)
    n_loc = WIN_R * GRID_W
    lo, hi = _half_masks(BF16)
    lane = lax.broadcasted_iota(jnp.int32, (1, LANES), 1)
    for hp in range(H_B // 2):
        cols = slice(hp * LANES, (hp + 1) * LANES)
        qp = q_ref[0, :, cols]
        kp = k_ref[0, pl.ds(start, n_loc), cols]
        vp = v_ref[0, pl.ds(start, n_loc), cols]
        kcp = kc_ref[0, :, cols]
        vcp = vc_ref[0, :, cols]
        halves = []
        for e, msk in enumerate((lo, hi)):
            qm = qp * msk
            s_loc = lax.dot_general(qm, kp, _NT, preferred_element_type=F32) + tab_ref[0, 2 * hp + e]
            s_ctx = lax.dot_general(qm, kcp, _NT, preferred_element_type=F32)
            (p_loc, p_ctx), inv = _softmax_parts([s_loc, s_ctx])
            o = (jnp.dot(p_loc.astype(BF16), vp, preferred_element_type=F32)
                 + jnp.dot(p_ctx.astype(BF16), vcp, preferred_element_type=F32))
            halves.append(o * inv)
        o_ref[0, :, cols] = jnp.where(lane < HD, halves[0], halves[1]).astype(BF16)


def _na_bias_table(rpb):
    case = jnp.arange(WIN_R)[:, None, None, None]
    j = jnp.arange(WIN_R)[None, :, None, None]
    c = jnp.arange(GRID_W)[None, None, :, None]
    kc = jnp.arange(GRID_W)[None, None, None, :]
    cstart = jnp.clip(c - WIN_C // 2, 0, GRID_W - WIN_C)
    inside = (kc >= cstart) & (kc < cstart + WIN_C)
    rel_r = jnp.clip(j - case + (WIN_R - 1), 0, 2 * WIN_R - 2)
    rel_c = jnp.clip(kc - c + (WIN_C - 1), 0, 2 * WIN_C - 2)
    shape = (WIN_R, WIN_R, GRID_W, GRID_W)
    bias = rpb[:, jnp.broadcast_to(rel_r, shape), jnp.broadcast_to(rel_c, shape)]
    tab = jnp.where(jnp.broadcast_to(inside, shape)[None], bias, NEG_BIG)
    tab = jnp.transpose(tab, (1, 0, 3, 2, 4))
    return tab.reshape(WIN_R, H_B, GRID_W, WIN_R * GRID_W).astype(F32)


def _na_attn(qb, kb, vb, kb_c, vb_c, tab):
    b, t, _ = qb.shape
    rows = t // GRID_W
    n_ctx = kb_c.shape[1]
    lat = pl.BlockSpec((1, t, D_B), lambda bi, r: (bi, 0, 0))
    ctx = pl.BlockSpec((1, n_ctx, D_B), lambda bi, r: (bi, 0, 0))
    row = pl.BlockSpec((1, GRID_W, D_B), lambda bi, r: (bi, r, 0))

    def tab_map(bi, r):
        return (r - jnp.clip(r - WIN_R // 2, 0, rows - WIN_R), 0, 0, 0)

    return pl.pallas_call(
        functools.partial(_na_kernel, rows=rows),
        out_shape=jax.ShapeDtypeStruct((b, t, D_B), BF16),
        grid=(b, rows),
        in_specs=[row, lat, lat, ctx, ctx,
                  pl.BlockSpec((1, H_B, GRID_W, WIN_R * GRID_W), tab_map)],
        out_specs=row,
        compiler_params=_cparams(2), name="na_attn",
    )(qb, kb, vb, kb_c, vb_c, tab)


def _ctx_attn_kernel(lp_ref, g_ref, qa_ref, ka_ref, va_ref, qb_ref, kb_ref, vb_ref,
                     oa_ref, ob_ref, *, lambda_init):
    lam = _lambda_full(lp_ref, lambda_init)
    lo, hi = _half_masks(BF16)
    lane = lax.broadcasted_iota(jnp.int32, (1, LANES), 1)
    for h in range(H_A):
        cols = slice(h * LANES, (h + 1) * LANES)
        q, k, v = qa_ref[0, :, cols], ka_ref[0, :, cols], va_ref[0, :, cols]
        probs = []
        for msk in (lo, hi):
            s = lax.dot_general(q * msk, k, _NT, preferred_element_type=F32)
            (p,), inv = _softmax_parts([s])
            probs.append(p * inv)
        w = (probs[0] - lam * probs[1]).astype(BF16)
        o = jnp.dot(w, v, preferred_element_type=F32)
        oa_ref[0, :, cols] = _head_rms(o, g_ref[...], lambda_init).astype(BF16)
    for hp in range(H_B // 2):
        cols = slice(hp * LANES, (hp + 1) * LANES)
        q, k, v = qb_ref[0, :, cols], kb_ref[0, :, cols], vb_ref[0, :, cols]
        halves = []
        for msk in (lo, hi):
            s = lax.dot_general(q * msk, k, _NT, preferred_element_type=F32)
            (p,), inv = _softmax_parts([s])
            halves.append(jnp.dot((p * inv).astype(BF16), v, preferred_element_type=F32))
        ob_ref[0, :, cols] = jnp.where(lane < HD, halves[0], halves[1]).astype(BF16)


def _ctx_attn(lp, g, qa, ka, va, qb, kb, vb, lambda_init):
    b, n, _ = qa.shape
    blk = pl.BlockSpec((1, n, D_A), lambda bi: (bi, 0, 0))
    out = jax.ShapeDtypeStruct((b, n, D_A), BF16)
    return pl.pallas_call(
        functools.partial(_ctx_attn_kernel, lambda_init=lambda_init),
        out_shape=(out, out),
        grid=(b,),
        in_specs=[pl.BlockSpec((4, HD), lambda bi: (0, 0)),
                  pl.BlockSpec((1, 2 * HD), lambda bi: (0, 0))] + [blk] * 6,
        out_specs=(blk, blk),
        compiler_params=_cparams(1), name="ctx_attn",
    )(lp, g, qa, ka, va, qb, kb, vb)


def _outproj_ln_kernel(oa_ref, ob_ref, w_ref, x_ref, gate_ref, g_ref, b_ref, o_ref):
    o = (jnp.dot(oa_ref[0], w_ref[:D_A], preferred_element_type=F32)
         + jnp.dot(ob_ref[0], w_ref[D_A:], preferred_element_type=F32))
    y = ALPHA * x_ref[0] + gate_ref[0] * o
    o_ref[0] = _layer_norm(y, g_ref[...], b_ref[...])


def _outproj_ln(oa, ob, w, x, gate, ln_g, ln_b, tm):
    b, t, _ = x.shape
    half = pl.BlockSpec((1, tm, D_A), lambda bi, i: (bi, i, 0))
    full = pl.BlockSpec((1, tm, D_MODEL), lambda bi, i: (bi, i, 0))
    vec = pl.BlockSpec((1, 1, D_MODEL), lambda bi, i: (bi, 0, 0))
    par = pl.BlockSpec((1, D_MODEL), lambda bi, i: (0, 0))
    return pl.pallas_call(
        _outproj_ln_kernel,
        out_shape=jax.ShapeDtypeStruct((b, t, D_MODEL), F32),
        grid=(b, t // tm),
        in_specs=[half, half, pl.BlockSpec((D_MODEL, D_MODEL), lambda bi, i: (0, 0)),
                  full, vec, par, par],
        out_specs=full,
        compiler_params=_cparams(2), name="outproj_ln",
    )(oa, ob, w, x, gate, ln_g, ln_b)


def _fill_modulated(h_ref, x_ref, xp_ref, xn_ref, sc_ref, sh_ref, tm):
    i = pl.program_id(1)
    last = pl.num_programs(1) - 1
    scale = 1.0 + sc_ref[0]
    shift = sh_ref[0]
    h_ref[HALO:HALO + tm] = (x_ref[0] * scale + shift).astype(BF16)
    prev = xp_ref[0] * scale + shift
    nxt = xn_ref[0] * scale + shift
    h_ref[0:HALO] = jnp.where(i > 0, prev, 0.0).astype(BF16)
    h_ref[HALO + tm:] = jnp.where(i < last, nxt, 0.0).astype(BF16)


def _conv_ffn_kernel(x_ref, xp_ref, xn_ref, sc_ref, sh_ref, gate_ref,
                     wg_ref, wv_ref, cwg_ref, cbg_ref, cwv_ref, cbv_ref, wd_ref,
                     lg_ref, lb_ref, o_ref, h_ref, ug_ref, uv_ref, acc_ref, *, tm):
    _fill_modulated(h_ref, x_ref, xp_ref, xn_ref, sc_ref, sh_ref, tm)
    acc_ref[...] = jnp.zeros_like(acc_ref)

    def conv3(u_ref, cw, cb):
        return (cw[0:1] * u_ref[pl.ds(HALO - 1, tm), :] + cw[1:2] * u_ref[pl.ds(HALO, tm), :]
                + cw[2:3] * u_ref[pl.ds(HALO + 1, tm), :] + cb)

    def body(c, _):
        h = h_ref[...]
        ug_ref[...] = jnp.dot(h, wg_ref[c], preferred_element_type=F32)
        uv_ref[...] = jnp.dot(h, wv_ref[c], preferred_element_type=F32)
        g = conv3(ug_ref, cwg_ref[c], cbg_ref[c])
        v = conv3(uv_ref, cwv_ref[c], cbv_ref[c])
        a = (jax.nn.gelu(g, approximate=True) * v).astype(BF16)
        acc_ref[...] += jnp.dot(a, wd_ref[c], preferred_element_type=F32)
        return 0

    lax.fori_loop(0, N_FF_CHUNKS, body, 0)
    y = ALPHA * x_ref[0] + gate_ref[0] * acc_ref[...]
    o_ref[0] = _layer_norm(y, lg_ref[...], lb_ref[...])


def _halo_specs(t, tm):
    nb = t // HALO
    per = tm // HALO
    main = pl.BlockSpec((1, tm, D_MODEL), lambda bi, i: (bi, i, 0))
    prev = pl.BlockSpec((1, HALO, D_MODEL), lambda bi, i: (bi, jnp.maximum(i * per - 1, 0), 0))
    nxt = pl.BlockSpec((1, HALO, D_MODEL), lambda bi, i: (bi, jnp.minimum((i + 1) * per, nb - 1), 0))
    return main, prev, nxt


def _const_spec(shape):
    nd = len(shape)
    return pl.BlockSpec(shape, lambda bi, i: (0,) * nd)


def _conv_ffn_ln(x, sc, sh, gate, fw, ln_g, ln_b, tm):
    b, t, _ = x.shape
    main, prev, nxt = _halo_specs(t, tm)
    vec = pl.BlockSpec((1, 1, D_MODEL), lambda bi, i: (bi, 0, 0))
    par = _const_spec((1, D_MODEL))
    wg, wv, cwg, cbg, cwv, cbv, wd = fw
    return pl.pallas_call(
        functools.partial(_conv_ffn_kernel, tm=tm),
        out_shape=jax.ShapeDtypeStruct((b, t, D_MODEL), F32),
        grid=(b, t // tm),
        in_specs=[main, prev, nxt, vec, vec, vec,
                  _const_spec(wg.shape), _const_spec(wv.shape),
                  _const_spec(cwg.shape), _const_spec(cbg.shape),
                  _const_spec(cwv.shape), _const_spec(cbv.shape),
                  _const_spec(wd.shape), par, par],
        out_specs=main,
        scratch_shapes=[pltpu.VMEM((tm + 2 * HALO, D_MODEL), BF16),
                        pltpu.VMEM((tm + 2 * HALO, FF_CHUNK), F32),
                        pltpu.VMEM((tm + 2 * HALO, FF_CHUNK), F32),
                        pltpu.VMEM((tm, D_MODEL), F32)],
        compiler_params=_cparams(2), name="conv_ffn_ln",
    )(x, x, x, sc, sh, gate, wg, wv, cwg, cbg, cwv, cbv, wd, ln_g, ln_b)


def _prep_ffn(w_up, conv_w, conv_b, w_down):
    def chunks_cols(a):
        return jnp.transpose(a.reshape(a.shape[0], N_FF_CHUNKS, FF_CHUNK), (1, 0, 2))
    wg = chunks_cols(w_up[:, :D_FF]).astype(BF16)
    wv = chunks_cols(w_up[:, D_FF:]).astype(BF16)
    cwg = chunks_cols(conv_w[:, :D_FF])
    cwv = chunks_cols(conv_w[:, D_FF:])
    cbg = chunks_cols(conv_b[None, :D_FF])
    cbv = chunks_cols(conv_b[None, D_FF:])
    wd = w_down.reshape(N_FF_CHUNKS, FF_CHUNK, D_MODEL).astype(BF16)
    return wg, wv, cwg, cbg, cwv, cbv, wd


def _rg_coeffs(xb, wgate_ref, bgate_ref, ap_ref, a_ref, b_ref):
    ap = ap_ref[...]
    neg = -ap
    sp = jnp.maximum(neg, 0.0) + jnp.log1p(jnp.exp(-jnp.abs(neg)))
    for n in range(N_RG_BLOCKS):
        cols = slice(n * RG_BW, (n + 1) * RG_BW)
        xn = xb[:, cols]
        z = jnp.dot(xn.astype(BF16), wgate_ref[n], preferred_element_type=F32) + bgate_ref[n]
        gate_a = jax.nn.sigmoid(z[:, :RG_BW])
        gate_x = jax.nn.sigmoid(z[:, RG_BW:])
        log_a = -RG_C * gate_a * sp[:, cols]
        a = jnp.exp(log_a)
        one_minus_a2 = -jnp.tanh(log_a) * (1.0 + a * a)
        a_ref[:, cols] = a
        b_ref[:, cols] = gate_x * xn * jnp.sqrt(one_minus_a2)


def _scan_tile(a_ref, b_ref, r_ref, h_in, tt, reverse):
    n_groups = tt // 8
    row = lax.broadcasted_iota(jnp.int32, (8, D_MODEL), 0)

    def body(gi, h):
        g = (n_groups - 1 - gi) if reverse else gi
        off = pl.multiple_of(g * 8, 8)
        a = a_ref[pl.ds(off, 8), :]
        b = b_ref[pl.ds(off, 8), :]
        for k in (1, 2, 4):
            shift = (8 - k) if reverse else k
            msk = (row < 8 - k) if reverse else (row >= k)
            a_sh = pltpu.roll(a, shift, 0)
            b_sh = pltpu.roll(b, shift, 0)
            b = jnp.where(msk, a * b_sh + b, b)
            a = jnp.where(msk, a * a_sh, a)
        hg = a * h + b
        r_ref[pl.ds(off, 8), :] = hg
        return hg[0:1] if reverse else hg[7:8]

    return lax.fori_loop(0, n_groups, body, h_in)


def _rnn_in_kernel(x_ref, xp_ref, xn_ref, sc_ref, sh_ref, w_ref, cw_ref, cb_ref,
                   wgate_ref, bgate_ref, ap_ref, h0_ref,
                   r_ref, y_ref, xb_ref, h_ref, u_ref, a_ref, b_ref, carry_ref, *, tt, reverse):
    i = pl.program_id(1)

    @pl.when(i == 0)
    def _():
        carry_ref[...] = h0_ref[0]

    _fill_modulated_dir(h_ref, x_ref, xp_ref, xn_ref, sc_ref, sh_ref, tt, reverse)
    h = h_ref[...]
    y_ref[0] = jnp.dot(h[HALO:HALO + tt], w_ref[:, :D_MODEL], preferred_element_type=F32).astype(BF16)
    u_ref[...] = jnp.dot(h, w_ref[:, D_MODEL:], preferred_element_type=F32)
    cw = cw_ref[...]
    xb = (cw[0:1] * u_ref[pl.ds(HALO - 2, tt), :] + cw[1:2] * u_ref[pl.ds(HALO - 1, tt), :]
          + cw[2:3] * u_ref[pl.ds(HALO, tt), :] + cw[3:4] * u_ref[pl.ds(HALO + 1, tt), :]
          + cb_ref[...])
    xb_ref[0] = xb
    _rg_coeffs(xb, wgate_ref, bgate_ref, ap_ref, a_ref, b_ref)
    carry_ref[...] = _scan_tile(a_ref, b_ref, r_ref.at[0], carry_ref[...], tt, reverse)


def _fill_modulated_dir(h_ref, x_ref, xp_ref, xn_ref, sc_ref, sh_ref, tt, reverse):
    i = pl.program_id(1)
    last = pl.num_programs(1) - 1
    ti = (last - i) if reverse else i
    scale = 1.0 + sc_ref[0]
    shift = sh_ref[0]
    h_ref[HALO:HALO + tt] = (x_ref[0] * scale + shift).astype(BF16)
    prev = xp_ref[0] * scale + shift
    nxt = xn_ref[0] * scale + shift
    h_ref[0:HALO] = jnp.where(ti > 0, prev, 0.0).astype(BF16)
    h_ref[HALO + tt:] = jnp.where(ti < last, nxt, 0.0).astype(BF16)


def _rnn_in(x, sc, sh, w_in, conv_w, conv_b, wgate, bgate, ap, h0, tt, reverse):
    b, t, _ = x.shape
    nt = t // tt
    nb = t // HALO
    per = tt // HALO

    def tix(i):
        return (nt - 1 - i) if reverse else i

    main = pl.BlockSpec((1, tt, D_MODEL), lambda bi, i: (bi, tix(i), 0))
    prev = pl.BlockSpec((1, HALO, D_MODEL), lambda bi, i: (bi, jnp.maximum(tix(i) * per - 1, 0), 0))
    nxt = pl.BlockSpec((1, HALO, D_MODEL), lambda bi, i: (bi, jnp.minimum((tix(i) + 1) * per, nb - 1), 0))
    vec = pl.BlockSpec((1, 1, D_MODEL), lambda bi, i: (bi, 0, 0))
    return pl.pallas_call(
        functools.partial(_rnn_in_kernel, tt=tt, reverse=reverse),
        out_shape=(jax.ShapeDtypeStruct((b, t, D_MODEL), F32),
                   jax.ShapeDtypeStruct((b, t, D_MODEL), BF16),
                   jax.ShapeDtypeStruct((b, t, D_MODEL), F32)),
        grid=(b, nt),
        in_specs=[main, prev, nxt, vec, vec,
                  _const_spec(w_in.shape), _const_spec(conv_w.shape), _const_spec((1, D_MODEL)),
                  _const_spec(wgate.shape), _const_spec(bgate.shape), _const_spec((1, D_MODEL)), vec],
        out_specs=(main, main, main),
        scratch_shapes=[pltpu.VMEM((tt + 2 * HALO, D_MODEL), BF16),
                        pltpu.VMEM((tt + 2 * HALO, D_MODEL), F32),
                        pltpu.VMEM((tt, D_MODEL), F32),
                        pltpu.VMEM((tt, D_MODEL), F32),
                        pltpu.VMEM((1, D_MODEL), F32)],
        compiler_params=_cparams(2), name="rnn_in_rev" if reverse else "rnn_in_fwd",
    )(x, x, x, sc, sh, w_in, conv_w, conv_b, wgate, bgate, ap, h0)


def _rnn_out_kernel(xb_ref, y_ref, rf_ref, x_ref, gate_ref, wgate_ref, bgate_ref, ap_ref, h0_ref,
                    w_ref, lg_ref, lb_ref, o_ref, a_ref, b_ref, rb_ref, carry_ref, *, tt):
    i = pl.program_id(1)

    @pl.when(i == 0)
    def _():
        carry_ref[...] = h0_ref[0]

    _rg_coeffs(xb_ref[0], wgate_ref, bgate_ref, ap_ref, a_ref, b_ref)
    carry_ref[...] = _scan_tile(a_ref, b_ref, rb_ref, carry_ref[...], tt, True)
    r = rf_ref[0] + rb_ref[...]
    m = (r * jax.nn.gelu(y_ref[0].astype(F32), approximate=True)).astype(BF16)
    o = jnp.dot(m, w_ref[...], preferred_element_type=F32)
    y = ALPHA * x_ref[0] + gate_ref[0] * o
    o_ref[0] = _layer_norm(y, lg_ref[...], lb_ref[...])


def _rnn_out(xb, y, rf, x, gate, wgate, bgate, ap, h0, w_out, ln_g, ln_b, tt):
    b, t, _ = x.shape
    nt = t // tt
    main = pl.BlockSpec((1, tt, D_MODEL), lambda bi, i: (bi, nt - 1 - i, 0))
    vec = pl.BlockSpec((1, 1, D_MODEL), lambda bi, i: (bi, 0, 0))
    par = _const_spec((1, D_MODEL))
    return pl.pallas_call(
        functools.partial(_rnn_out_kernel, tt=tt),
        out_shape=jax.ShapeDtypeStruct((b, t, D_MODEL), F32),
        grid=(b, nt),
        in_specs=[main, main, main, main, vec,
                  _const_spec(wgate.shape), _const_spec(bgate.shape), par, vec,
                  _const_spec(w_out.shape), par, par],
        out_specs=main,
        scratch_shapes=[pltpu.VMEM((tt, D_MODEL), F32)] * 3 + [pltpu.VMEM((1, D_MODEL), F32)],
        compiler_params=_cparams(2), name="rnn_out",
    )(xb, y, rf, x, gate, wgate, bgate, ap, h0, w_out, ln_g, ln_b)


def _rope_tables(n_tok):
    t = jnp.arange(n_tok)
    row = (t // GRID_W).astype(F32)[:, None]
    col = (t % GRID_W).astype(F32)[:, None]
    nf = HD // 4
    inv = 1.0 / (ROPE_THETA ** (jnp.arange(nf, dtype=F32) / nf))
    ang = jnp.concatenate([row * inv, row * inv, col * inv, col * inv], -1)
    sign = jnp.where((jnp.arange(HD) % (HD // 2)) < nf, -1.0, 1.0)
    reps = D_A // HD
    return jnp.tile(jnp.cos(ang), (1, reps)), jnp.tile(jnp.sin(ang) * sign, (1, reps))


def _tile(n, pref):
    return pref if n % pref == 0 else n


def kernel(x, c, ctx, c_ctx, ada_w, ada_b, ln1_g, ln1_b, ln2_g, ln2_b, ffn_w_up, ffn_conv_w, ffn_conv_b, ffn_w_down, att_w_in, att_w_out, diff_lq1, diff_lk1, diff_lq2, diff_lk2, diff_subln_g, na_rpb, rnn_w_in, rnn_conv_w, rnn_conv_b, rg_a_param, rg_wa, rg_ba, rg_wx, rg_bx, rnn_w_out):
    bsz, t, _ = x.shape
    n_ctx = ctx.shape[1]
    assert t % GRID_W == 0 and t // GRID_W >= WIN_R and n_ctx % HALO == 0

    n_rows = ((bsz + 1 + 7) // 8) * 8
    cc = jnp.zeros((n_rows, D_MODEL), F32).at[:bsz].set(c).at[bsz].set(c_ctx)
    mod = _ada_mod(cc, ada_w, ada_b)

    def mods(l):
        lat = mod[l, :bsz].reshape(bsz, 1, 6, D_MODEL)
        cx = jnp.broadcast_to(mod[l, bsz].reshape(1, 1, 6, D_MODEL), (bsz, 1, 6, D_MODEL))
        return ([lat[:, :, k] for k in range(6)], [cx[:, :, k] for k in range(6)])

    tm = _tile(t, 512)
    tmc = _tile(n_ctx, 512)

    (sh1, sc1, g1, sh2, sc2, g2), (csh1, csc1, cg1, csh2, csc2, cg2) = mods(0)
    lambda_init = 0.8 - 0.6 * math.exp(-0.3 * 0)
    w_in = att_w_in[0].astype(BF16)
    w_out = att_w_out[0].astype(BF16)
    cos_t, sin_t = _rope_tables(t)
    qa, ka, va, qb, kb, vb = _proj_attn(x, sc1, sh1, w_in, cos_t, sin_t, True, tm)
    qa_c, ka_c, va_c, qb_c, kb_c, vb_c = _proj_attn(
        ctx, csc1, csh1, w_in, cos_t[:n_ctx], sin_t[:n_ctx], False, tmc)
    lp = jnp.stack([diff_lq1[0], diff_lk1[0], diff_lq2[0], diff_lk2[0]])
    sub_g = diff_subln_g[0].reshape(1, 2 * HD)
    oa = _diff_attn(lp, sub_g, qa, ka_c, va_c, ka, va, lambda_init, _tile(t, 256), _tile(t, 512))
    ob = _na_attn(qb, kb, vb, kb_c, vb_c, _na_bias_table(na_rpb[0]))
    oa_c, ob_c = _ctx_attn(lp, sub_g, qa_c, ka_c, va_c, qb_c, kb_c, vb_c, lambda_init)
    l1g, l1b = ln1_g[0].reshape(1, D_MODEL), ln1_b[0].reshape(1, D_MODEL)
    l2g, l2b = ln2_g[0].reshape(1, D_MODEL), ln2_b[0].reshape(1, D_MODEL)
    x = _outproj_ln(oa, ob, w_out, x, g1, l1g, l1b, tm)
    ctx = _outproj_ln(oa_c, ob_c, w_out, ctx, cg1, l1g, l1b, tmc)
    fw = _prep_ffn(ffn_w_up[0], ffn_conv_w[0], ffn_conv_b[0], ffn_w_down[0])
    x = _conv_ffn_ln(x, sc2, sh2, g2, fw, l2g, l2b, tm)
    ctx = _conv_ffn_ln(ctx, csc2, csh2, cg2, fw, l2g, l2b, tmc)

    (sh1, sc1, g1, sh2, sc2, g2), (csh1, csc1, _, _, _, _) = mods(1)
    w_in = rnn_w_in[0].astype(BF16)
    w_out = rnn_w_out[0].astype(BF16)
    conv_b = rnn_conv_b[0].reshape(1, D_MODEL)
    wgate = [jnp.concatenate([rg_wa[0, d], rg_wx[0, d]], axis=-1).astype(BF16) for d in range(2)]
    bgate = [jnp.concatenate([rg_ba[0, d], rg_bx[0, d]], axis=-1).reshape(N_RG_BLOCKS, 1, 2 * RG_BW)
             for d in range(2)]
    ap = [rg_a_param[0, d].reshape(1, D_MODEL) for d in range(2)]
    zeros = jnp.zeros((bsz, 1, D_MODEL), F32)
    tt = _tile(t, 256)
    ttc = _tile(n_ctx, 256)
    rc_f, _, _ = _rnn_in(ctx, csc1, csh1, w_in, rnn_conv_w[0], conv_b, wgate[0], bgate[0], ap[0],
                         zeros, ttc, False)
    rc_b, _, _ = _rnn_in(ctx, csc1, csh1, w_in, rnn_conv_w[0], conv_b, wgate[1], bgate[1], ap[1],
                         zeros, ttc, True)
    h0_f = rc_f[:, n_ctx - 1:n_ctx]
    h0_b = rc_b[:, 0:1]
    r_f, y, xb = _rnn_in(x, sc1, sh1, w_in, rnn_conv_w[0], conv_b, wgate[0], bgate[0], ap[0],
                         h0_f, tt, False)
    l1g, l1b = ln1_g[1].reshape(1, D_MODEL), ln1_b[1].reshape(1, D_MODEL)
    l2g, l2b = ln2_g[1].reshape(1, D_MODEL), ln2_b[1].reshape(1, D_MODEL)
    x = _rnn_out(xb, y, r_f, x, g1, wgate[1], bgate[1], ap[1], h0_b, w_out, l1g, l1b, tt)
    fw = _prep_ffn(ffn_w_up[1], ffn_conv_w[1], ffn_conv_b[1], ffn_w_down[1])
    x = _conv_ffn_ln(x, sc2, sh2, g2, fw, l2g, l2b, tm)
    return x
```

```python
import functools
import math

import jax
import jax.numpy as jnp
from jax import lax
from jax.experimental import pallas as pl
from jax.experimental.pallas import tpu as pltpu

F32 = jnp.float32
BF16 = jnp.bfloat16

D_MODEL = 1024
DEPTH = 2
GRID_W = 64
HD = 64
D_A = D_MODEL // 2
D_B = D_MODEL - D_A
H_A = D_A // (2 * HD)
H_B = D_B // HD
WIN_R = 8
WIN_C = 16
ROPE_THETA = 10000.0
N_RG_BLOCKS = 8
RG_BW = D_MODEL // N_RG_BLOCKS
RG_C = 8.0
CONV_RNN = 4
D_FF = ((8 * D_MODEL // 3 + 127) // 128) * 128
CONV_FFN = 3
ALPHA = (2 * DEPTH) ** 0.25
NORM_EPS = 1e-5
NEG_BIG = -1e30
Q_SCALE = HD ** -0.5 * math.log2(math.e)

LANES = 128
HALO = 16
DENOM_ROWS = 16
FF_CHUNK = 256
N_FF_CHUNKS = D_FF // FF_CHUNK
VMEM_LIMIT = 56 * 1024 * 1024

_NT = (((1,), (1,)), ((), ()))


def _cparams(n_axes):
    return pltpu.CompilerParams(dimension_semantics=("arbitrary",) * n_axes,
                                vmem_limit_bytes=VMEM_LIMIT)


def _layer_norm(y, g, b):
    mu = jnp.mean(y, axis=-1, keepdims=True)
    yc = y - mu
    var = jnp.mean(yc * yc, axis=-1, keepdims=True)
    return yc * lax.rsqrt(var + NORM_EPS) * g + b


def _softmax_parts(s_list):
    m = s_list[0].max(axis=-1, keepdims=True)
    for s in s_list[1:]:
        m = jnp.maximum(m, s.max(axis=-1, keepdims=True))
    ps = [jnp.exp2(s - m) for s in s_list]
    l = ps[0].sum(axis=-1, keepdims=True)
    for p in ps[1:]:
        l = l + p.sum(axis=-1, keepdims=True)
    return ps, 1.0 / l


def _half_masks(dtype):
    lane = lax.broadcasted_iota(jnp.int32, (1, LANES), 1)
    lo = (lane < HD).astype(dtype)
    return lo, (1 - lo).astype(dtype)


def _ada_kernel(c_ref, w_ref, b_ref, o_ref):
    c = c_ref[...]
    a = c * jax.nn.sigmoid(c)
    o_ref[0] = jnp.dot(a, w_ref[0], preferred_element_type=F32,
                       precision=lax.Precision.HIGHEST) + b_ref[0]


def _ada_mod(cc, ada_w, ada_b):
    n = cc.shape[0]
    tn = 1536
    return pl.pallas_call(
        _ada_kernel,
        out_shape=jax.ShapeDtypeStruct((DEPTH, n, 6 * D_MODEL), F32),
        grid=(DEPTH, 6 * D_MODEL // tn),
        in_specs=[pl.BlockSpec((n, D_MODEL), lambda l, j: (0, 0)),
                  pl.BlockSpec((1, D_MODEL, tn), lambda l, j: (l, 0, j)),
                  pl.BlockSpec((1, 1, tn), lambda l, j: (l, 0, j))],
        out_specs=pl.BlockSpec((1, n, tn), lambda l, j: (l, 0, j)),
        compiler_params=_cparams(2), name="ada_mod",
    )(cc, ada_w, ada_b.reshape(DEPTH, 1, 6 * D_MODEL))


def _proj_attn_kernel(x_ref, sc_ref, sh_ref, w_ref, cos_ref, sin_ref,
                      qa_ref, ka_ref, va_ref, qb_ref, kb_ref, vb_ref, vt_ref, *, rope):
    h = (x_ref[0] * (1.0 + sc_ref[0]) + sh_ref[0]).astype(BF16)
    outs = (qa_ref, ka_ref, va_ref, qb_ref, kb_ref, vb_ref)
    for j, o_ref in enumerate(outs):
        r = jnp.dot(h, w_ref[:, j * D_A:(j + 1) * D_A], preferred_element_type=F32)
        if rope and j < 2:
            lane = lax.broadcasted_iota(jnp.int32, r.shape, 1)
            first_half = (lane % (HD // 2)) < (HD // 4)
            rot = jnp.where(first_half, pltpu.roll(r, D_A - HD // 4, 1), pltpu.roll(r, HD // 4, 1))
            r = r * cos_ref[...] + rot * sin_ref[...]
        if j in (0, 3):
            r = r * Q_SCALE
        if j == 2:
            vt_ref[...] = r
            o_ref[0] = vt_ref[...].T.astype(BF16)
        else:
            o_ref[0] = r.astype(BF16)


def _proj_attn(x, sc, sh, w, cos_t, sin_t, rope, tm):
    b, t, _ = x.shape
    out = jax.ShapeDtypeStruct((b, t, D_A), BF16)
    out_t = jax.ShapeDtypeStruct((b, D_A, t), BF16)
    vec = pl.BlockSpec((1, 1, D_MODEL), lambda bi, i: (bi, 0, 0))
    blk = pl.BlockSpec((1, tm, D_A), lambda bi, i: (bi, i, 0))
    blk_t = pl.BlockSpec((1, D_A, tm), lambda bi, i: (bi, 0, i))
    tab = pl.BlockSpec((tm, D_A), lambda bi, i: (i, 0))
    return pl.pallas_call(
        functools.partial(_proj_attn_kernel, rope=rope),
        out_shape=(out, out, out_t, out, out, out),
        grid=(b, t // tm),
        in_specs=[pl.BlockSpec((1, tm, D_MODEL), lambda bi, i: (bi, i, 0)), vec, vec,
                  pl.BlockSpec((D_MODEL, 6 * D_A), lambda bi, i: (0, 0)), tab, tab],
        out_specs=(blk, blk, blk_t, blk, blk, blk),
        scratch_shapes=[pltpu.VMEM((tm, D_A), F32)],
        compiler_params=_cparams(2), name="proj_attn",
    )(x, sc, sh, w, cos_t, sin_t)


def _lambda_full(lp_ref, lambda_init):
    lp = lp_ref[...]
    s1 = jnp.sum(lp[0:1] * lp[1:2], axis=-1, keepdims=True)
    s2 = jnp.sum(lp[2:3] * lp[3:4], axis=-1, keepdims=True)
    return jnp.exp(s1) - jnp.exp(s2) + lambda_init


def _head_rms(o, g, lambda_init):
    o = o * lax.rsqrt(jnp.mean(o * o, axis=-1, keepdims=True) + NORM_EPS)
    return o * g * (1.0 - lambda_init)


def _diff_attn_kernel(lp_ref, g_ref, q_ref, kc_ref, vct_ref, kl_ref, vlt_ref, o_ref,
                      s_ref, p_ref, acc_ref, *, tk, n_chunks, n_ctx, lambda_init):
    lam = _lambda_full(lp_ref, lambda_init)
    q = q_ref[0]
    lo, hi = _half_masks(BF16)
    qs = (q * lo, q * hi)
    tq = q.shape[0]
    chunks = [(kc_ref, vct_ref, 0, n_ctx)] + [(kl_ref, vlt_ref, j * tk, tk) for j in range(n_chunks)]
    n_all = len(chunks)
    acc_ref[...] = jnp.zeros_like(acc_ref)

    def scores(c):
        k_ref, _, off, size = chunks[c]
        k = k_ref[0, off:off + size, :]
        for mi in range(2):
            s_ref[c % 2, mi, :size, :] = lax.dot_general(k, qs[mi], _NT, preferred_element_type=F32)

    def softmax(c, m_old):
        size = chunks[c][3]
        m_out, a_out = [], []
        for mi in range(2):
            s = s_ref[c % 2, mi, :size, :]
            m_new = jnp.maximum(m_old[mi], s.max(axis=0, keepdims=True))
            a_out.append(jnp.exp2(m_old[mi] - m_new))
            p_ref[c % 2, mi, :size, :] = jnp.exp2(s - m_new).astype(BF16)
            m_out.append(m_new)
        return m_out, a_out

    def values(c, a):
        _, vt_ref, off, size = chunks[c]
        vext = jnp.concatenate([vt_ref[0, :, off:off + size], jnp.ones((DENOM_ROWS, size), BF16)], axis=0)
        for mi in range(2):
            pv = jnp.dot(vext, p_ref[c % 2, mi, :size, :], preferred_element_type=F32)
            acc_ref[mi] = a[mi] * acc_ref[mi] + pv

    m = [jnp.full((1, tq), NEG_BIG, F32)] * 2
    a_next = None
    for i in range(-2, n_all):
        a_cur = a_next
        if i + 2 < n_all:
            scores(i + 2)
        if 0 <= i + 1 < n_all:
            m, a_next = softmax(i + 1, m)
        if i >= 0:
            values(i, a_cur)

    outs = []
    for mi in range(2):
        acc = acc_ref[mi]
        outs.append(acc[:2 * HD] / acc[2 * HD:2 * HD + 1])
    o = (outs[0] - lam * outs[1]).T
    o_ref[0] = _head_rms(o, g_ref[...], lambda_init).astype(BF16)


def _diff_attn(lp, g, qa, ka_c, vat_c, ka, vat, lambda_init, tq, tk):
    b, t, _ = qa.shape
    n_ctx = ka_c.shape[1]
    assert n_ctx <= tk and n_ctx % LANES == 0
    q_blk = pl.BlockSpec((1, tq, LANES), lambda bi, h, i: (bi, i, h))
    return pl.pallas_call(
        functools.partial(_diff_attn_kernel, tk=tk, n_chunks=t // tk, n_ctx=n_ctx, lambda_init=lambda_init),
        out_shape=jax.ShapeDtypeStruct((b, t, D_A), BF16),
        grid=(b, H_A, t // tq),
        in_specs=[pl.BlockSpec((4, HD), lambda bi, h, i: (0, 0)),
                  pl.BlockSpec((1, 2 * HD), lambda bi, h, i: (0, 0)),
                  q_blk,
                  pl.BlockSpec((1, n_ctx, LANES), lambda bi, h, i: (bi, 0, h)),
                  pl.BlockSpec((1, LANES, n_ctx), lambda bi, h, i: (bi, h, 0)),
                  pl.BlockSpec((1, t, LANES), lambda bi, h, i: (bi, 0, h)),
                  pl.BlockSpec((1, LANES, t), lambda bi, h, i: (bi, h, 0))],
        out_specs=q_blk,
        scratch_shapes=[pltpu.VMEM((2, 2, tk, tq), F32),
                        pltpu.VMEM((2, 2, tk, tq), BF16),
                        pltpu.VMEM((2, 2 * HD + DENOM_ROWS, tq), F32)],
        compiler_params=_cparams(3), name="diff_attn",
    )(lp, g, qa, ka_c, vat_c, ka, vat)


def _na_kernel(q_ref, k_ref, v_ref, kc_ref, vc_ref, tab_ref, o_ref, *, rows):
    r = pl.program_id(1)
    rs = jnp.clip(r - WIN_R // 2, 0, rows - WIN_R)
    start = pl.multiple_of(rs * GRID_W, GRID_W)
    n_loc = WIN_R * GRID_W
    lo, hi = _half_masks(BF16)
    lane = lax.broadcasted_iota(jnp.int32, (1, LANES), 1)
    rel0 = rs - r + (WIN_R - 1)
    for hp in range(H_B // 2):
        cols = slice(hp * LANES, (hp + 1) * LANES)
        qp = q_ref[0, :, cols]
        kp = k_ref[0, pl.ds(start, n_loc), cols]
        vp = v_ref[0, pl.ds(start, n_loc), cols]
        kcp = kc_ref[0, :, cols]
        vcp = vc_ref[0, :, cols]
        halves = []
        for e, msk in enumerate((lo, hi)):
            qm = qp * msk
            bias = jnp.concatenate([tab_ref[2 * hp + e, rel0 + 2 * jj] for jj in range(WIN_R // 2)], axis=1)
            s_loc = lax.dot_general(qm, kp, _NT, preferred_element_type=F32) + bias
            s_ctx = lax.dot_general(qm, kcp, _NT, preferred_element_type=F32)
            (p_loc, p_ctx), inv = _softmax_parts([s_loc, s_ctx])
            o = (jnp.dot(p_loc.astype(BF16), vp, preferred_element_type=F32)
                 + jnp.dot(p_ctx.astype(BF16), vcp, preferred_element_type=F32))
            halves.append(o * inv)
        o_ref[0, :, cols] = jnp.where(lane < HD, halves[0], halves[1]).astype(BF16)


def _na_bias_table(rpb):
    c = jnp.arange(GRID_W)[:, None]
    kc = jnp.arange(GRID_W)[None, :]
    cstart = jnp.clip(c - WIN_C // 2, 0, GRID_W - WIN_C)
    inside = (kc >= cstart) & (kc < cstart + WIN_C)
    rel_c = jnp.clip(kc - c + (WIN_C - 1), 0, 2 * WIN_C - 2)
    band = jnp.where(inside, rpb[:, :, rel_c] * math.log2(math.e), NEG_BIG)
    return jnp.concatenate([band[:, :-1], band[:, 1:]], axis=-1).astype(F32)


def _na_attn(qb, kb, vb, kb_c, vb_c, tab):
    b, t, _ = qb.shape
    rows = t // GRID_W
    n_ctx = kb_c.shape[1]
    lat = pl.BlockSpec((1, t, D_B), lambda bi, r: (bi, 0, 0))
    ctx = pl.BlockSpec((1, n_ctx, D_B), lambda bi, r: (bi, 0, 0))
    row = pl.BlockSpec((1, GRID_W, D_B), lambda bi, r: (bi, r, 0))

    return pl.pallas_call(
        functools.partial(_na_kernel, rows=rows),
        out_shape=jax.ShapeDtypeStruct((b, t, D_B), BF16),
        grid=(b, rows),
        in_specs=[row, lat, lat, ctx, ctx,
                  pl.BlockSpec(tab.shape, lambda bi, r: (0, 0, 0, 0))],
        out_specs=row,
        compiler_params=_cparams(2), name="na_attn",
    )(qb, kb, vb, kb_c, vb_c, tab)


def _ctx_attn_kernel(lp_ref, g_ref, qa_ref, ka_ref, vat_ref, qb_ref, kb_ref, vb_ref,
                     oa_ref, ob_ref, *, lambda_init):
    lam = _lambda_full(lp_ref, lambda_init)
    lo, hi = _half_masks(BF16)
    lane = lax.broadcasted_iota(jnp.int32, (1, LANES), 1)
    for h in range(H_A):
        cols = slice(h * LANES, (h + 1) * LANES)
        q, k, vt = qa_ref[0, :, cols], ka_ref[0, :, cols], vat_ref[0, cols, :]
        probs = []
        for msk in (lo, hi):
            s = lax.dot_general(q * msk, k, _NT, preferred_element_type=F32)
            (p,), inv = _softmax_parts([s])
            probs.append(p * inv)
        w = (probs[0] - lam * probs[1]).astype(BF16)
        o = lax.dot_general(w, vt, _NT, preferred_element_type=F32)
        oa_ref[0, :, cols] = _head_rms(o, g_ref[...], lambda_init).astype(BF16)
    for hp in range(H_B // 2):
        cols = slice(hp * LANES, (hp + 1) * LANES)
        q, k, v = qb_ref[0, :, cols], kb_ref[0, :, cols], vb_ref[0, :, cols]
        halves = []
        for msk in (lo, hi):
            s = lax.dot_general(q * msk, k, _NT, preferred_element_type=F32)
            (p,), inv = _softmax_parts([s])
            halves.append(jnp.dot((p * inv).astype(BF16), v, preferred_element_type=F32))
        ob_ref[0, :, cols] = jnp.where(lane < HD, halves[0], halves[1]).astype(BF16)


def _ctx_attn(lp, g, qa, ka, vat, qb, kb, vb, lambda_init):
    b, n, _ = qa.shape
    blk = pl.BlockSpec((1, n, D_A), lambda bi: (bi, 0, 0))
    blk_t = pl.BlockSpec((1, D_A, n), lambda bi: (bi, 0, 0))
    out = jax.ShapeDtypeStruct((b, n, D_A), BF16)
    return pl.pallas_call(
        functools.partial(_ctx_attn_kernel, lambda_init=lambda_init),
        out_shape=(out, out),
        grid=(b,),
        in_specs=[pl.BlockSpec((4, HD), lambda bi: (0, 0)),
                  pl.BlockSpec((1, 2 * HD), lambda bi: (0, 0)),
                  blk, blk, blk_t, blk, blk, blk],
        out_specs=(blk, blk),
        compiler_params=_cparams(1), name="ctx_attn",
    )(lp, g, qa, ka, vat, qb, kb, vb)


def _outproj_ln_kernel(oa_ref, ob_ref, w_ref, x_ref, gate_ref, g_ref, b_ref, o_ref):
    o = (jnp.dot(oa_ref[0], w_ref[:D_A], preferred_element_type=F32)
         + jnp.dot(ob_ref[0], w_ref[D_A:], preferred_element_type=F32))
    y = ALPHA * x_ref[0] + gate_ref[0] * o
    o_ref[0] = _layer_norm(y, g_ref[...], b_ref[...])


def _outproj_ln(oa, ob, w, x, gate, ln_g, ln_b, tm):
    b, t, _ = x.shape
    half = pl.BlockSpec((1, tm, D_A), lambda bi, i: (bi, i, 0))
    full = pl.BlockSpec((1, tm, D_MODEL), lambda bi, i: (bi, i, 0))
    vec = pl.BlockSpec((1, 1, D_MODEL), lambda bi, i: (bi, 0, 0))
    par = pl.BlockSpec((1, D_MODEL), lambda bi, i: (0, 0))
    return pl.pallas_call(
        _outproj_ln_kernel,
        out_shape=jax.ShapeDtypeStruct((b, t, D_MODEL), F32),
        grid=(b, t // tm),
        in_specs=[half, half, pl.BlockSpec((D_MODEL, D_MODEL), lambda bi, i: (0, 0)),
                  full, vec, par, par],
        out_specs=full,
        compiler_params=_cparams(2), name="outproj_ln",
    )(oa, ob, w, x, gate, ln_g, ln_b)


def _fill_modulated(h_ref, x_ref, xp_ref, xn_ref, sc_ref, sh_ref, tm):
    i = pl.program_id(1)
    last = pl.num_programs(1) - 1
    scale = 1.0 + sc_ref[0]
    shift = sh_ref[0]
    h_ref[HALO:HALO + tm] = (x_ref[0] * scale + shift).astype(BF16)
    prev = xp_ref[0] * scale + shift
    nxt = xn_ref[0] * scale + shift
    h_ref[0:HALO] = jnp.where(i > 0, prev, 0.0).astype(BF16)
    h_ref[HALO + tm:] = jnp.where(i < last, nxt, 0.0).astype(BF16)


def _conv_ffn_kernel(x_ref, xp_ref, xn_ref, sc_ref, sh_ref, gate_ref,
                     wg_ref, wv_ref, cwg_ref, cbg_ref, cwv_ref, cbv_ref, wd_ref,
                     lg_ref, lb_ref, o_ref, h_ref, ug_ref, uv_ref, acc_ref, *, tm):
    _fill_modulated(h_ref, x_ref, xp_ref, xn_ref, sc_ref, sh_ref, tm)
    acc_ref[...] = jnp.zeros_like(acc_ref)

    def conv3(u_ref, cw, cb):
        return (cw[0:1] * u_ref[pl.ds(HALO - 1, tm), :] + cw[1:2] * u_ref[pl.ds(HALO, tm), :]
                + cw[2:3] * u_ref[pl.ds(HALO + 1, tm), :] + cb)

    def body(c, _):
        h = h_ref[...]
        ug_ref[...] = jnp.dot(h, wg_ref[c], preferred_element_type=F32)
        uv_ref[...] = jnp.dot(h, wv_ref[c], preferred_element_type=F32)
        g = conv3(ug_ref, cwg_ref[c], cbg_ref[c])
        v = conv3(uv_ref, cwv_ref[c], cbv_ref[c])
        a = (jax.nn.gelu(g, approximate=True) * v).astype(BF16)
        acc_ref[...] += jnp.dot(a, wd_ref[c], preferred_element_type=F32)
        return 0

    lax.fori_loop(0, N_FF_CHUNKS, body, 0)
    y = ALPHA * x_ref[0] + gate_ref[0] * acc_ref[...]
    o_ref[0] = _layer_norm(y, lg_ref[...], lb_ref[...])


def _halo_specs(t, tm):
    nb = t // HALO
    per = tm // HALO
    main = pl.BlockSpec((1, tm, D_MODEL), lambda bi, i: (bi, i, 0))
    prev = pl.BlockSpec((1, HALO, D_MODEL), lambda bi, i: (bi, jnp.maximum(i * per - 1, 0), 0))
    nxt = pl.BlockSpec((1, HALO, D_MODEL), lambda bi, i: (bi, jnp.minimum((i + 1) * per, nb - 1), 0))
    return main, prev, nxt


def _const_spec(shape):
    nd = len(shape)
    return pl.BlockSpec(shape, lambda bi, i: (0,) * nd)


def _conv_ffn_ln(x, sc, sh, gate, fw, ln_g, ln_b, tm):
    b, t, _ = x.shape
    main, prev, nxt = _halo_specs(t, tm)
    vec = pl.BlockSpec((1, 1, D_MODEL), lambda bi, i: (bi, 0, 0))
    par = _const_spec((1, D_MODEL))
    wg, wv, cwg, cbg, cwv, cbv, wd = fw
    return pl.pallas_call(
        functools.partial(_conv_ffn_kernel, tm=tm),
        out_shape=jax.ShapeDtypeStruct((b, t, D_MODEL), F32),
        grid=(b, t // tm),
        in_specs=[main, prev, nxt, vec, vec, vec,
                  _const_spec(wg.shape), _const_spec(wv.shape),
                  _const_spec(cwg.shape), _const_spec(cbg.shape),
                  _const_spec(cwv.shape), _const_spec(cbv.shape),
                  _const_spec(wd.shape), par, par],
        out_specs=main,
        scratch_shapes=[pltpu.VMEM((tm + 2 * HALO, D_MODEL), BF16),
                        pltpu.VMEM((tm + 2 * HALO, FF_CHUNK), F32),
                        pltpu.VMEM((tm + 2 * HALO, FF_CHUNK), F32),
                        pltpu.VMEM((tm, D_MODEL), F32)],
        compiler_params=_cparams(2), name="conv_ffn_ln",
    )(x, x, x, sc, sh, gate, wg, wv, cwg, cbg, cwv, cbv, wd, ln_g, ln_b)


def _prep_ffn(w_up, conv_w, conv_b, w_down):
    def chunks_cols(a):
        return jnp.transpose(a.reshape(a.shape[0], N_FF_CHUNKS, FF_CHUNK), (1, 0, 2))
    wg = chunks_cols(w_up[:, :D_FF]).astype(BF16)
    wv = chunks_cols(w_up[:, D_FF:]).astype(BF16)
    cwg = chunks_cols(conv_w[:, :D_FF])
    cwv = chunks_cols(conv_w[:, D_FF:])
    cbg = chunks_cols(conv_b[None, :D_FF])
    cbv = chunks_cols(conv_b[None, D_FF:])
    wd = w_down.reshape(N_FF_CHUNKS, FF_CHUNK, D_MODEL).astype(BF16)
    return wg, wv, cwg, cbg, cwv, cbv, wd


def _rg_coeffs(xb, wgate_ref, bgate_ref, ap_ref, a_ref, b_ref):
    ap = ap_ref[...]
    neg = -ap
    sp = jnp.maximum(neg, 0.0) + jnp.log1p(jnp.exp(-jnp.abs(neg)))
    for n in range(N_RG_BLOCKS):
        cols = slice(n * RG_BW, (n + 1) * RG_BW)
        xn = xb[:, cols]
        z = jnp.dot(xn.astype(BF16), wgate_ref[n], preferred_element_type=F32) + bgate_ref[n]
        gate_a = jax.nn.sigmoid(z[:, :RG_BW])
        gate_x = jax.nn.sigmoid(z[:, RG_BW:])
        log_a = -RG_C * gate_a * sp[:, cols]
        a = jnp.exp(log_a)
        one_minus_a2 = -jnp.tanh(log_a) * (1.0 + a * a)
        a_ref[:, cols] = a
        b_ref[:, cols] = gate_x * xn * jnp.sqrt(one_minus_a2)


def _scan_tile(a_ref, b_ref, r_ref, h_in, tt, reverse):
    n_groups = tt // 8
    row = lax.broadcasted_iota(jnp.int32, (8, D_MODEL), 0)

    def body(gi, h):
        g = (n_groups - 1 - gi) if reverse else gi
        off = pl.multiple_of(g * 8, 8)
        a = a_ref[pl.ds(off, 8), :]
        b = b_ref[pl.ds(off, 8), :]
        for k in (1, 2, 4):
            shift = (8 - k) if reverse else k
            msk = (row < 8 - k) if reverse else (row >= k)
            a_sh = pltpu.roll(a, shift, 0)
            b_sh = pltpu.roll(b, shift, 0)
            b = jnp.where(msk, a * b_sh + b, b)
            a = jnp.where(msk, a * a_sh, a)
        hg = a * h + b
        r_ref[pl.ds(off, 8), :] = hg
        return hg[0:1] if reverse else hg[7:8]

    return lax.fori_loop(0, n_groups, body, h_in)


def _rnn_in_kernel(x_ref, xp_ref, xn_ref, sc_ref, sh_ref, w_ref, cw_ref, cb_ref,
                   wgate_ref, bgate_ref, ap_ref, h0_ref,
                   r_ref, y_ref, xb_ref, h_ref, u_ref, a_ref, b_ref, carry_ref, *, tt, reverse):
    i = pl.program_id(1)

    @pl.when(i == 0)
    def _():
        carry_ref[...] = h0_ref[0]

    _fill_modulated_dir(h_ref, x_ref, xp_ref, xn_ref, sc_ref, sh_ref, tt, reverse)
    h = h_ref[...]
    y_ref[0] = jnp.dot(h[HALO:HALO + tt], w_ref[:, :D_MODEL], preferred_element_type=F32).astype(BF16)
    u_ref[...] = jnp.dot(h, w_ref[:, D_MODEL:], preferred_element_type=F32)
    cw = cw_ref[...]
    xb = (cw[0:1] * u_ref[pl.ds(HALO - 2, tt), :] + cw[1:2] * u_ref[pl.ds(HALO - 1, tt), :]
          + cw[2:3] * u_ref[pl.ds(HALO, tt), :] + cw[3:4] * u_ref[pl.ds(HALO + 1, tt), :]
          + cb_ref[...])
    xb_ref[0] = xb
    _rg_coeffs(xb, wgate_ref, bgate_ref, ap_ref, a_ref, b_ref)
    carry_ref[...] = _scan_tile(a_ref, b_ref, r_ref.at[0], carry_ref[...], tt, reverse)


def _fill_modulated_dir(h_ref, x_ref, xp_ref, xn_ref, sc_ref, sh_ref, tt, reverse):
    i = pl.program_id(1)
    last = pl.num_programs(1) - 1
    ti = (last - i) if reverse else i
    scale = 1.0 + sc_ref[0]
    shift = sh_ref[0]
    h_ref[HALO:HALO + tt] = (x_ref[0] * scale + shift).astype(BF16)
    prev = xp_ref[0] * scale + shift
    nxt = xn_ref[0] * scale + shift
    h_ref[0:HALO] = jnp.where(ti > 0, prev, 0.0).astype(BF16)
    h_ref[HALO + tt:] = jnp.where(ti < last, nxt, 0.0).astype(BF16)


def _rnn_in(x, sc, sh, w_in, conv_w, conv_b, wgate, bgate, ap, h0, tt, reverse):
    b, t, _ = x.shape
    nt = t // tt
    nb = t // HALO
    per = tt // HALO

    def tix(i):
        return (nt - 1 - i) if reverse else i

    main = pl.BlockSpec((1, tt, D_MODEL), lambda bi, i: (bi, tix(i), 0))
    prev = pl.BlockSpec((1, HALO, D_MODEL), lambda bi, i: (bi, jnp.maximum(tix(i) * per - 1, 0), 0))
    nxt = pl.BlockSpec((1, HALO, D_MODEL), lambda bi, i: (bi, jnp.minimum((tix(i) + 1) * per, nb - 1), 0))
    vec = pl.BlockSpec((1, 1, D_MODEL), lambda bi, i: (bi, 0, 0))
    return pl.pallas_call(
        functools.partial(_rnn_in_kernel, tt=tt, reverse=reverse),
        out_shape=(jax.ShapeDtypeStruct((b, t, D_MODEL), F32),
                   jax.ShapeDtypeStruct((b, t, D_MODEL), BF16),
                   jax.ShapeDtypeStruct((b, t, D_MODEL), F32)),
        grid=(b, nt),
        in_specs=[main, prev, nxt, vec, vec,
                  _const_spec(w_in.shape), _const_spec(conv_w.shape), _const_spec((1, D_MODEL)),
                  _const_spec(wgate.shape), _const_spec(bgate.shape), _const_spec((1, D_MODEL)), vec],
        out_specs=(main, main, main),
        scratch_shapes=[pltpu.VMEM((tt + 2 * HALO, D_MODEL), BF16),
                        pltpu.VMEM((tt + 2 * HALO, D_MODEL), F32),
                        pltpu.VMEM((tt, D_MODEL), F32),
                        pltpu.VMEM((tt, D_MODEL), F32),
                        pltpu.VMEM((1, D_MODEL), F32)],
        compiler_params=_cparams(2), name="rnn_in_rev" if reverse else "rnn_in_fwd",
    )(x, x, x, sc, sh, w_in, conv_w, conv_b, wgate, bgate, ap, h0)


def _rnn_out_kernel(xb_ref, y_ref, rf_ref, x_ref, gate_ref, wgate_ref, bgate_ref, ap_ref, h0_ref,
                    w_ref, lg_ref, lb_ref, o_ref, a_ref, b_ref, rb_ref, carry_ref, *, tt):
    i = pl.program_id(1)

    @pl.when(i == 0)
    def _():
        carry_ref[...] = h0_ref[0]

    _rg_coeffs(xb_ref[0], wgate_ref, bgate_ref, ap_ref, a_ref, b_ref)
    carry_ref[...] = _scan_tile(a_ref, b_ref, rb_ref, carry_ref[...], tt, True)
    r = rf_ref[0] + rb_ref[...]
    m = (r * jax.nn.gelu(y_ref[0].astype(F32), approximate=True)).astype(BF16)
    o = jnp.dot(m, w_ref[...], preferred_element_type=F32)
    y = ALPHA * x_ref[0] + gate_ref[0] * o
    o_ref[0] = _layer_norm(y, lg_ref[...], lb_ref[...])


def _rnn_out(xb, y, rf, x, gate, wgate, bgate, ap, h0, w_out, ln_g, ln_b, tt):
    b, t, _ = x.shape
    nt = t // tt
    main = pl.BlockSpec((1, tt, D_MODEL), lambda bi, i: (bi, nt - 1 - i, 0))
    vec = pl.BlockSpec((1, 1, D_MODEL), lambda bi, i: (bi, 0, 0))
    par = _const_spec((1, D_MODEL))
    return pl.pallas_call(
        functools.partial(_rnn_out_kernel, tt=tt),
        out_shape=jax.ShapeDtypeStruct((b, t, D_MODEL), F32),
        grid=(b, nt),
        in_specs=[main, main, main, main, vec,
                  _const_spec(wgate.shape), _const_spec(bgate.shape), par, vec,
                  _const_spec(w_out.shape), par, par],
        out_specs=main,
        scratch_shapes=[pltpu.VMEM((tt, D_MODEL), F32)] * 3 + [pltpu.VMEM((1, D_MODEL), F32)],
        compiler_params=_cparams(2), name="rnn_out",
    )(xb, y, rf, x, gate, wgate, bgate, ap, h0, w_out, ln_g, ln_b)


def _rope_tables(n_tok):
    t = jnp.arange(n_tok)
    row = (t // GRID_W).astype(F32)[:, None]
    col = (t % GRID_W).astype(F32)[:, None]
    nf = HD // 4
    inv = 1.0 / (ROPE_THETA ** (jnp.arange(nf, dtype=F32) / nf))
    ang = jnp.concatenate([row * inv, row * inv, col * inv, col * inv], -1)
    sign = jnp.where((jnp.arange(HD) % (HD // 2)) < nf, -1.0, 1.0)
    reps = D_A // HD
    return jnp.tile(jnp.cos(ang), (1, reps)), jnp.tile(jnp.sin(ang) * sign, (1, reps))


def _tile(n, pref):
    return pref if n % pref == 0 else n


def kernel(x, c, ctx, c_ctx, ada_w, ada_b, ln1_g, ln1_b, ln2_g, ln2_b, ffn_w_up, ffn_conv_w, ffn_conv_b, ffn_w_down, att_w_in, att_w_out, diff_lq1, diff_lk1, diff_lq2, diff_lk2, diff_subln_g, na_rpb, rnn_w_in, rnn_conv_w, rnn_conv_b, rg_a_param, rg_wa, rg_ba, rg_wx, rg_bx, rnn_w_out):
    bsz, t, _ = x.shape
    n_ctx = ctx.shape[1]
    assert t % GRID_W == 0 and t // GRID_W >= WIN_R and n_ctx % HALO == 0

    n_rows = ((bsz + 1 + 7) // 8) * 8
    cc = jnp.zeros((n_rows, D_MODEL), F32).at[:bsz].set(c).at[bsz].set(c_ctx)
    mod = _ada_mod(cc, ada_w, ada_b)

    def mods(l):
        lat = mod[l, :bsz].reshape(bsz, 1, 6, D_MODEL)
        cx = jnp.broadcast_to(mod[l, bsz].reshape(1, 1, 6, D_MODEL), (bsz, 1, 6, D_MODEL))
        return ([lat[:, :, k] for k in range(6)], [cx[:, :, k] for k in range(6)])

    tm = _tile(t, 512)
    tmc = _tile(n_ctx, 512)

    (sh1, sc1, g1, sh2, sc2, g2), (csh1, csc1, cg1, csh2, csc2, cg2) = mods(0)
    lambda_init = 0.8 - 0.6 * math.exp(-0.3 * 0)
    w_in = att_w_in[0].astype(BF16)
    w_out = att_w_out[0].astype(BF16)
    cos_t, sin_t = _rope_tables(t)
    qa, ka, va, qb, kb, vb = _proj_attn(x, sc1, sh1, w_in, cos_t, sin_t, True, tm)
    qa_c, ka_c, va_c, qb_c, kb_c, vb_c = _proj_attn(
        ctx, csc1, csh1, w_in, cos_t[:n_ctx], sin_t[:n_ctx], False, tmc)
    lp = jnp.stack([diff_lq1[0], diff_lk1[0], diff_lq2[0], diff_lk2[0]])
    sub_g = diff_subln_g[0].reshape(1, 2 * HD)
    oa = _diff_attn(lp, sub_g, qa, ka_c, va_c, ka, va, lambda_init, _tile(t, 256), _tile(t, 512))
    ob = _na_attn(qb, kb, vb, kb_c, vb_c, _na_bias_table(na_rpb[0]))
    oa_c, ob_c = _ctx_attn(lp, sub_g, qa_c, ka_c, va_c, qb_c, kb_c, vb_c, lambda_init)
    l1g, l1b = ln1_g[0].reshape(1, D_MODEL), ln1_b[0].reshape(1, D_MODEL)
    l2g, l2b = ln2_g[0].reshape(1, D_MODEL), ln2_b[0].reshape(1, D_MODEL)
    x = _outproj_ln(oa, ob, w_out, x, g1, l1g, l1b, tm)
    ctx = _outproj_ln(oa_c, ob_c, w_out, ctx, cg1, l1g, l1b, tmc)
    fw = _prep_ffn(ffn_w_up[0], ffn_conv_w[0], ffn_conv_b[0], ffn_w_down[0])
    x = _conv_ffn_ln(x, sc2, sh2, g2, fw, l2g, l2b, tm)
    ctx = _conv_ffn_ln(ctx, csc2, csh2, cg2, fw, l2g, l2b, tmc)

    (sh1, sc1, g1, sh2, sc2, g2), (csh1, csc1, _, _, _, _) = mods(1)
    w_in = rnn_w_in[0].astype(BF16)
    w_out = rnn_w_out[0].astype(BF16)
    conv_b = rnn_conv_b[0].reshape(1, D_MODEL)
    wgate = [jnp.concatenate([rg_wa[0, d], rg_wx[0, d]], axis=-1).astype(BF16) for d in range(2)]
    bgate = [jnp.concatenate([rg_ba[0, d], rg_bx[0, d]], axis=-1).reshape(N_RG_BLOCKS, 1, 2 * RG_BW)
             for d in range(2)]
    ap = [rg_a_param[0, d].reshape(1, D_MODEL) for d in range(2)]
    zeros = jnp.zeros((bsz, 1, D_MODEL), F32)
    tt = _tile(t, 256)
    ttc = _tile(n_ctx, 256)
    rc_f, _, _ = _rnn_in(ctx, csc1, csh1, w_in, rnn_conv_w[0], conv_b, wgate[0], bgate[0], ap[0],
                         zeros, ttc, False)
    rc_b, _, _ = _rnn_in(ctx, csc1, csh1, w_in, rnn_conv_w[0], conv_b, wgate[1], bgate[1], ap[1],
                         zeros, ttc, True)
    h0_f = rc_f[:, n_ctx - 1:n_ctx]
    h0_b = rc_b[:, 0:1]
    r_f, y, xb = _rnn_in(x, sc1, sh1, w_in, rnn_conv_w[0], conv_b, wgate[0], bgate[0], ap[0],
                         h0_f, tt, False)
    l1g, l1b = ln1_g[1].reshape(1, D_MODEL), ln1_b[1].reshape(1, D_MODEL)
    l2g, l2b = ln2_g[1].reshape(1, D_MODEL), ln2_b[1].reshape(1, D_MODEL)
    x = _rnn_out(xb, y, r_f, x, g1, wgate[1], bgate[1], ap[1], h0_b, w_out, l1g, l1b, tt)
    fw = _prep_ffn(ffn_w_up[1], ffn_conv_w[1], ffn_conv_b[1], ffn_w_down[1])
    x = _conv_ffn_ln(x, sc2, sh2, g2, fw, l2g, l2b, tm)
    return x
```

```python
import functools
import math

import jax
import jax.numpy as jnp
from jax import lax
from jax.experimental import pallas as pl
from jax.experimental.pallas import tpu as pltpu

F32 = jnp.float32
BF16 = jnp.bfloat16

D_MODEL = 1024
DEPTH = 2
GRID_W = 64
HD = 64
D_A = D_MODEL // 2
D_B = D_MODEL - D_A
H_A = D_A // (2 * HD)
H_B = D_B // HD
WIN_R = 8
WIN_C = 16
ROPE_THETA = 10000.0
N_RG_BLOCKS = 8
RG_BW = D_MODEL // N_RG_BLOCKS
RG_C = 8.0
CONV_RNN = 4
D_FF = ((8 * D_MODEL // 3 + 127) // 128) * 128
CONV_FFN = 3
ALPHA = (2 * DEPTH) ** 0.25
NORM_EPS = 1e-5
NEG_BIG = -1e30
Q_SCALE = HD ** -0.5 * math.log2(math.e)

LANES = 128
HALO = 16
DENOM_ROWS = 16
FF_CHUNK = 256
N_FF_CHUNKS = D_FF // FF_CHUNK
VMEM_LIMIT = 56 * 1024 * 1024

_NT = (((1,), (1,)), ((), ()))


def _cparams(n_axes, flags=None):
    return pltpu.CompilerParams(dimension_semantics=("arbitrary",) * n_axes,
                                vmem_limit_bytes=VMEM_LIMIT, flags=flags)


def _layer_norm(y, g, b):
    mu = jnp.mean(y, axis=-1, keepdims=True)
    yc = y - mu
    var = jnp.mean(yc * yc, axis=-1, keepdims=True)
    return yc * lax.rsqrt(var + NORM_EPS) * g + b


def _softmax_parts(s_list):
    m = s_list[0].max(axis=-1, keepdims=True)
    for s in s_list[1:]:
        m = jnp.maximum(m, s.max(axis=-1, keepdims=True))
    ps = [jnp.exp2(s - m) for s in s_list]
    l = ps[0].sum(axis=-1, keepdims=True)
    for p in ps[1:]:
        l = l + p.sum(axis=-1, keepdims=True)
    return ps, 1.0 / l


def _half_masks(dtype):
    lane = lax.broadcasted_iota(jnp.int32, (1, LANES), 1)
    lo = (lane < HD).astype(dtype)
    return lo, (1 - lo).astype(dtype)


def _ada_kernel(c_ref, w_ref, b_ref, o_ref):
    c = c_ref[...]
    a = c * jax.nn.sigmoid(c)
    o_ref[0] = jnp.dot(a, w_ref[0], preferred_element_type=F32,
                       precision=lax.Precision.HIGHEST) + b_ref[0]


def _ada_mod(cc, ada_w, ada_b):
    n = cc.shape[0]
    tn = 1536
    return pl.pallas_call(
        _ada_kernel,
        out_shape=jax.ShapeDtypeStruct((DEPTH, n, 6 * D_MODEL), F32),
        grid=(DEPTH, 6 * D_MODEL // tn),
        in_specs=[pl.BlockSpec((n, D_MODEL), lambda l, j: (0, 0)),
                  pl.BlockSpec((1, D_MODEL, tn), lambda l, j: (l, 0, j)),
                  pl.BlockSpec((1, 1, tn), lambda l, j: (l, 0, j))],
        out_specs=pl.BlockSpec((1, n, tn), lambda l, j: (l, 0, j)),
        compiler_params=_cparams(2), name="ada_mod",
    )(cc, ada_w, ada_b.reshape(DEPTH, 1, 6 * D_MODEL))


def _proj_attn_kernel(x_ref, sc_ref, sh_ref, w_ref, cos_ref, sin_ref,
                      qa_ref, ka_ref, va_ref, qb_ref, kb_ref, vb_ref, vt_ref, *, rope):
    h = (x_ref[0] * (1.0 + sc_ref[0]) + sh_ref[0]).astype(BF16)
    outs = (qa_ref, ka_ref, va_ref, qb_ref, kb_ref, vb_ref)
    for j, o_ref in enumerate(outs):
        r = jnp.dot(h, w_ref[:, j * D_A:(j + 1) * D_A], preferred_element_type=F32)
        if rope and j < 2:
            lane = lax.broadcasted_iota(jnp.int32, r.shape, 1)
            first_half = (lane % (HD // 2)) < (HD // 4)
            rot = jnp.where(first_half, pltpu.roll(r, D_A - HD // 4, 1), pltpu.roll(r, HD // 4, 1))
            r = r * cos_ref[...] + rot * sin_ref[...]
        if j in (0, 3):
            r = r * Q_SCALE
        if j in (2, 5):
            vt_ref[...] = r
            o_ref[0] = vt_ref[...].T.astype(BF16)
        else:
            o_ref[0] = r.astype(BF16)


def _proj_attn(x, sc, sh, w, cos_t, sin_t, rope, tm):
    b, t, _ = x.shape
    out = jax.ShapeDtypeStruct((b, t, D_A), BF16)
    out_t = jax.ShapeDtypeStruct((b, D_A, t), BF16)
    vec = pl.BlockSpec((1, 1, D_MODEL), lambda bi, i: (bi, 0, 0))
    blk = pl.BlockSpec((1, tm, D_A), lambda bi, i: (bi, i, 0))
    blk_t = pl.BlockSpec((1, D_A, tm), lambda bi, i: (bi, 0, i))
    tab = pl.BlockSpec((tm, D_A), lambda bi, i: (i, 0))
    return pl.pallas_call(
        functools.partial(_proj_attn_kernel, rope=rope),
        out_shape=(out, out, out_t, out, out, out_t),
        grid=(b, t // tm),
        in_specs=[pl.BlockSpec((1, tm, D_MODEL), lambda bi, i: (bi, i, 0)), vec, vec,
                  pl.BlockSpec((D_MODEL, 6 * D_A), lambda bi, i: (0, 0)), tab, tab],
        out_specs=(blk, blk, blk_t, blk, blk, blk_t),
        scratch_shapes=[pltpu.VMEM((tm, D_A), F32)],
        compiler_params=_cparams(2), name="proj_attn",
    )(x, sc, sh, w, cos_t, sin_t)


def _lambda_full(lp_ref, lambda_init):
    lp = lp_ref[...]
    s1 = jnp.sum(lp[0:1] * lp[1:2], axis=-1, keepdims=True)
    s2 = jnp.sum(lp[2:3] * lp[3:4], axis=-1, keepdims=True)
    return jnp.exp(s1) - jnp.exp(s2) + lambda_init


def _head_rms(o, g, lambda_init):
    o = o * lax.rsqrt(jnp.mean(o * o, axis=-1, keepdims=True) + NORM_EPS)
    return o * g * (1.0 - lambda_init)


def _diff_attn_kernel(lp_ref, g_ref, q_ref, kc_ref, vct_ref, kl_ref, vlt_ref, o_ref,
                      s_ref, p_ref, acc_ref, *, tk, n_chunks, n_ctx, lambda_init):
    lam = _lambda_full(lp_ref, lambda_init)
    q = q_ref[0]
    lo, hi = _half_masks(BF16)
    qs = (q * lo, q * hi)
    tq = q.shape[0]
    chunks = [(kc_ref, vct_ref, 0, n_ctx)] + [(kl_ref, vlt_ref, j * tk, tk) for j in range(n_chunks)]
    n_all = len(chunks)
    acc_ref[...] = jnp.zeros_like(acc_ref)

    def scores(c):
        k_ref, _, off, size = chunks[c]
        k = k_ref[0, off:off + size, :]
        for mi in range(2):
            s_ref[c % 2, mi, :size, :] = lax.dot_general(k, qs[mi], _NT, preferred_element_type=F32)

    def softmax(c, m_old):
        size = chunks[c][3]
        m_out, a_out = [], []
        for mi in range(2):
            s = s_ref[c % 2, mi, :size, :]
            m_new = jnp.maximum(m_old[mi], s.max(axis=0, keepdims=True))
            a_out.append(jnp.exp2(m_old[mi] - m_new))
            p_ref[c % 2, mi, :size, :] = jnp.exp2(s - m_new).astype(BF16)
            m_out.append(m_new)
        return m_out, a_out

    def values(c, a):
        _, vt_ref, off, size = chunks[c]
        vext = jnp.concatenate([vt_ref[0, :, off:off + size], jnp.ones((DENOM_ROWS, size), BF16)], axis=0)
        for mi in range(2):
            pv = jnp.dot(vext, p_ref[c % 2, mi, :size, :], preferred_element_type=F32)
            acc_ref[mi] = a[mi] * acc_ref[mi] + pv

    m = [jnp.full((1, tq), NEG_BIG, F32)] * 2
    a_next = None
    for i in range(-2, n_all):
        a_cur = a_next
        if i + 2 < n_all:
            scores(i + 2)
        if 0 <= i + 1 < n_all:
            m, a_next = softmax(i + 1, m)
        if i >= 0:
            values(i, a_cur)

    outs = []
    for mi in range(2):
        acc = acc_ref[mi]
        outs.append(acc[:2 * HD] / acc[2 * HD:2 * HD + 1])
    o = (outs[0] - lam * outs[1]).T
    o_ref[0] = _head_rms(o, g_ref[...], lambda_init).astype(BF16)


def _diff_attn(lp, g, qa, ka_c, vat_c, ka, vat, lambda_init, tq, tk):
    b, t, _ = qa.shape
    n_ctx = ka_c.shape[1]
    assert n_ctx <= tk and n_ctx % LANES == 0
    q_blk = pl.BlockSpec((1, tq, LANES), lambda bi, h, i: (bi, i, h))
    return pl.pallas_call(
        functools.partial(_diff_attn_kernel, tk=tk, n_chunks=t // tk, n_ctx=n_ctx, lambda_init=lambda_init),
        out_shape=jax.ShapeDtypeStruct((b, t, D_A), BF16),
        grid=(b, H_A, t // tq),
        in_specs=[pl.BlockSpec((4, HD), lambda bi, h, i: (0, 0)),
                  pl.BlockSpec((1, 2 * HD), lambda bi, h, i: (0, 0)),
                  q_blk,
                  pl.BlockSpec((1, n_ctx, LANES), lambda bi, h, i: (bi, 0, h)),
                  pl.BlockSpec((1, LANES, n_ctx), lambda bi, h, i: (bi, h, 0)),
                  pl.BlockSpec((1, t, LANES), lambda bi, h, i: (bi, 0, h)),
                  pl.BlockSpec((1, LANES, t), lambda bi, h, i: (bi, h, 0))],
        out_specs=q_blk,
        scratch_shapes=[pltpu.VMEM((2, 2, tk, tq), F32),
                        pltpu.VMEM((2, 2, tk, tq), BF16),
                        pltpu.VMEM((2, 2 * HD + DENOM_ROWS, tq), F32)],
        compiler_params=_cparams(3), name="diff_attn",
    )(lp, g, qa, ka_c, vat_c, ka, vat)


NA_QROWS = 4
NA_KROWS = NA_QROWS + WIN_R
NA_REL_PAD = NA_QROWS
NA_REL_N = NA_KROWS + WIN_R + NA_QROWS - 2


def _na_kernel(q_ref, k_ref, vt_ref, kc_ref, vct_ref, tab_ref, o_ref, s_ref, p_ref, ot_ref, *, rows, n_ctx):
    g = pl.program_id(1)
    r0 = g * NA_QROWS
    ws = jnp.clip(r0 - WIN_R // 2, 0, rows - NA_KROWS)
    d0 = r0 - ws
    k_start = pl.multiple_of(ws * GRID_W, NA_QROWS * GRID_W)
    n_loc = NA_KROWS * GRID_W
    nq = NA_QROWS * GRID_W
    lo, hi = _half_masks(BF16)

    qr = lax.broadcasted_iota(jnp.int32, (1, nq), 1) // GRID_W
    rs = jnp.clip(r0 + qr - WIN_R // 2, 0, rows - WIN_R)
    row_mask = []
    for kr in range(NA_KROWS):
        valid = (ws + kr >= rs) & (ws + kr < rs + WIN_R)
        row_mask.append(jnp.where(valid, 0.0, NEG_BIG))

    def scores(h):
        hp, e = divmod(h, 2)
        cols = slice(hp * LANES, (hp + 1) * LANES)
        qm = q_ref[0, :, cols] * (lo, hi)[e]
        s_loc = lax.dot_general(k_ref[0, pl.ds(k_start, n_loc), cols], qm, _NT, preferred_element_type=F32)
        for kr in range(NA_KROWS):
            base = kr - d0 + (WIN_R - 1) + (NA_REL_PAD - 1)
            bias = jnp.concatenate([tab_ref[h, base - 2 * p] for p in range(NA_QROWS // 2)], axis=1)
            rows_kr = slice(kr * GRID_W, (kr + 1) * GRID_W)
            s_ref[h % 2, rows_kr, :] = s_loc[rows_kr] + bias + row_mask[kr]
        s_ref[h % 2, n_loc:, :] = lax.dot_general(kc_ref[0, :, cols], qm, _NT, preferred_element_type=F32)

    def softmax(h):
        s = s_ref[h % 2]
        m = s.max(axis=0, keepdims=True)
        p_ref[h % 2] = jnp.exp2(s - m).astype(BF16)

    def values(h):
        hrows = slice(h * HD, (h + 1) * HD)
        vt = jnp.concatenate([vt_ref[0, hrows, pl.ds(k_start, n_loc)], vct_ref[0, hrows, :]], axis=1)
        vext = jnp.concatenate([vt, jnp.ones((DENOM_ROWS, n_loc + n_ctx), BF16)], axis=0)
        o = jnp.dot(vext, p_ref[h % 2], preferred_element_type=F32)
        ot_ref[hrows, :] = o[:HD] / o[HD:HD + 1]

    for i in range(-2, H_B):
        if i + 2 < H_B:
            scores(i + 2)
        if 0 <= i + 1 < H_B:
            softmax(i + 1)
        if i >= 0:
            values(i)
    o_ref[0] = ot_ref[...].T.astype(BF16)


def _na_bias_table(rpb):
    c = jnp.arange(GRID_W)[None, :]
    kc = jnp.arange(GRID_W)[:, None]
    cstart = jnp.clip(c - WIN_C // 2, 0, GRID_W - WIN_C)
    inside = (kc >= cstart) & (kc < cstart + WIN_C)
    rel_c = jnp.clip(kc - c + (WIN_C - 1), 0, 2 * WIN_C - 2)
    band = jnp.where(inside, rpb[:, :, rel_c] * math.log2(math.e), NEG_BIG)
    hi_pad = NA_REL_N - NA_REL_PAD + 1 - (2 * WIN_R - 1)
    band = jnp.pad(band, ((0, 0), (NA_REL_PAD, hi_pad), (0, 0), (0, 0)), constant_values=NEG_BIG)
    return jnp.concatenate([band[:, 1:], band[:, :-1]], axis=-1).astype(F32)


def _na_attn(qb, kb, vbt, kb_c, vbt_c, tab):
    b, t, _ = qb.shape
    rows = t // GRID_W
    n_ctx = kb_c.shape[1]
    assert rows % NA_QROWS == 0 and rows >= NA_KROWS and tab.shape[1] == NA_REL_N
    nq = NA_QROWS * GRID_W
    n_keys = NA_KROWS * GRID_W + n_ctx
    qblk = pl.BlockSpec((1, nq, D_B), lambda bi, g: (bi, g, 0))
    return pl.pallas_call(
        functools.partial(_na_kernel, rows=rows, n_ctx=n_ctx),
        out_shape=jax.ShapeDtypeStruct((b, t, D_B), BF16),
        grid=(b, rows // NA_QROWS),
        in_specs=[qblk,
                  pl.BlockSpec((1, t, D_B), lambda bi, g: (bi, 0, 0)),
                  pl.BlockSpec((1, D_B, t), lambda bi, g: (bi, 0, 0)),
                  pl.BlockSpec((1, n_ctx, D_B), lambda bi, g: (bi, 0, 0)),
                  pl.BlockSpec((1, D_B, n_ctx), lambda bi, g: (bi, 0, 0)),
                  pl.BlockSpec(tab.shape, lambda bi, g: (0, 0, 0, 0))],
        out_specs=qblk,
        scratch_shapes=[pltpu.VMEM((2, n_keys, nq), F32),
                        pltpu.VMEM((2, n_keys, nq), BF16),
                        pltpu.VMEM((D_B, nq), F32)],
        compiler_params=_cparams(2), name="na_attn",
    )(qb, kb, vbt, kb_c, vbt_c, tab)


def _ctx_attn_kernel(lp_ref, g_ref, qa_ref, ka_ref, vat_ref, qb_ref, kb_ref, vbt_ref,
                     oa_ref, ob_ref, *, lambda_init):
    lam = _lambda_full(lp_ref, lambda_init)
    lo, hi = _half_masks(BF16)
    lane = lax.broadcasted_iota(jnp.int32, (1, LANES), 1)
    for h in range(H_A):
        cols = slice(h * LANES, (h + 1) * LANES)
        q, k, vt = qa_ref[0, :, cols], ka_ref[0, :, cols], vat_ref[0, cols, :]
        probs = []
        for msk in (lo, hi):
            s = lax.dot_general(q * msk, k, _NT, preferred_element_type=F32)
            (p,), inv = _softmax_parts([s])
            probs.append(p * inv)
        w = (probs[0] - lam * probs[1]).astype(BF16)
        o = lax.dot_general(w, vt, _NT, preferred_element_type=F32)
        oa_ref[0, :, cols] = _head_rms(o, g_ref[...], lambda_init).astype(BF16)
    for hp in range(H_B // 2):
        cols = slice(hp * LANES, (hp + 1) * LANES)
        q, k, vt = qb_ref[0, :, cols], kb_ref[0, :, cols], vbt_ref[0, cols, :]
        halves = []
        for msk in (lo, hi):
            s = lax.dot_general(q * msk, k, _NT, preferred_element_type=F32)
            (p,), inv = _softmax_parts([s])
            halves.append(lax.dot_general((p * inv).astype(BF16), vt, _NT, preferred_element_type=F32))
        ob_ref[0, :, cols] = jnp.where(lane < HD, halves[0], halves[1]).astype(BF16)


def _ctx_attn(lp, g, qa, ka, vat, qb, kb, vbt, lambda_init):
    b, n, _ = qa.shape
    blk = pl.BlockSpec((1, n, D_A), lambda bi: (bi, 0, 0))
    blk_t = pl.BlockSpec((1, D_A, n), lambda bi: (bi, 0, 0))
    out = jax.ShapeDtypeStruct((b, n, D_A), BF16)
    return pl.pallas_call(
        functools.partial(_ctx_attn_kernel, lambda_init=lambda_init),
        out_shape=(out, out),
        grid=(b,),
        in_specs=[pl.BlockSpec((4, HD), lambda bi: (0, 0)),
                  pl.BlockSpec((1, 2 * HD), lambda bi: (0, 0)),
                  blk, blk, blk_t, blk, blk, blk_t],
        out_specs=(blk, blk),
        compiler_params=_cparams(1), name="ctx_attn",
    )(lp, g, qa, ka, vat, qb, kb, vbt)


def _outproj_ln_kernel(oa_ref, ob_ref, w_ref, x_ref, gate_ref, g_ref, b_ref, o_ref):
    o = (jnp.dot(oa_ref[0], w_ref[:D_A], preferred_element_type=F32)
         + jnp.dot(ob_ref[0], w_ref[D_A:], preferred_element_type=F32))
    y = ALPHA * x_ref[0] + gate_ref[0] * o
    o_ref[0] = _layer_norm(y, g_ref[...], b_ref[...])


def _outproj_ln(oa, ob, w, x, gate, ln_g, ln_b, tm):
    b, t, _ = x.shape
    half = pl.BlockSpec((1, tm, D_A), lambda bi, i: (bi, i, 0))
    full = pl.BlockSpec((1, tm, D_MODEL), lambda bi, i: (bi, i, 0))
    vec = pl.BlockSpec((1, 1, D_MODEL), lambda bi, i: (bi, 0, 0))
    par = pl.BlockSpec((1, D_MODEL), lambda bi, i: (0, 0))
    return pl.pallas_call(
        _outproj_ln_kernel,
        out_shape=jax.ShapeDtypeStruct((b, t, D_MODEL), F32),
        grid=(b, t // tm),
        in_specs=[half, half, pl.BlockSpec((D_MODEL, D_MODEL), lambda bi, i: (0, 0)),
                  full, vec, par, par],
        out_specs=full,
        compiler_params=_cparams(2), name="outproj_ln",
    )(oa, ob, w, x, gate, ln_g, ln_b)


def _fill_modulated(h_ref, x_ref, xp_ref, xn_ref, sc_ref, sh_ref, tm):
    i = pl.program_id(1)
    last = pl.num_programs(1) - 1
    scale = 1.0 + sc_ref[0]
    shift = sh_ref[0]
    h_ref[HALO:HALO + tm] = (x_ref[0] * scale + shift).astype(BF16)
    prev = xp_ref[0] * scale + shift
    nxt = xn_ref[0] * scale + shift
    h_ref[0:HALO] = jnp.where(i > 0, prev, 0.0).astype(BF16)
    h_ref[HALO + tm:] = jnp.where(i < last, nxt, 0.0).astype(BF16)


def _conv_ffn_kernel(x_ref, xp_ref, xn_ref, sc_ref, sh_ref, gate_ref,
                     wg_ref, wv_ref, cwg_ref, cbg_ref, cwv_ref, cbv_ref, wd_ref,
                     lg_ref, lb_ref, o_ref, h_ref, ug_ref, uv_ref, a_ref, *, tm):
    _fill_modulated(h_ref, x_ref, xp_ref, xn_ref, sc_ref, sh_ref, tm)

    def conv3(u_ref, slot, cw, cb):
        u = u_ref[slot]
        rows, cols = u.shape
        u3 = u.reshape(rows // 8, 8, cols)
        sub = lax.broadcasted_iota(jnp.int32, u3.shape, 1)
        dn = pltpu.roll(u3, 1, 1)
        dn = jnp.where(sub == 0, jnp.concatenate([dn[-1:], dn[:-1]], axis=0), dn)
        up_ = pltpu.roll(u3, 7, 1)
        up_ = jnp.where(sub == 7, jnp.concatenate([up_[1:], up_[:1]], axis=0), up_)
        full = cw[0][None] * dn + cw[1][None] * u3 + cw[2][None] * up_ + cb[0][None]
        return full.reshape(rows, cols)[HALO:HALO + tm]

    def up(c, slot):
        h = h_ref[...]
        ug_ref[slot] = jnp.dot(h, wg_ref[c], preferred_element_type=F32)
        uv_ref[slot] = jnp.dot(h, wv_ref[c], preferred_element_type=F32)

    def act(c, slot):
        g = conv3(ug_ref, slot, cwg_ref[c], cbg_ref[c])
        v = conv3(uv_ref, slot, cwv_ref[c], cbv_ref[c])
        col = c * FF_CHUNK
        if not isinstance(c, int):
            col = pl.multiple_of(col, FF_CHUNK)
        a_ref[:, pl.ds(col, FF_CHUNK)] = (jax.nn.gelu(g, approximate=True) * v).astype(BF16)

    n = N_FF_CHUNKS
    up(0, 0)
    n_pairs = (n - 1) // 2

    def body(j, _):
        t = 1 + 2 * j
        up(t, 1)
        act(t - 1, 0)
        up(t + 1, 0)
        act(t, 1)
        return 0

    lax.fori_loop(0, n_pairs, body, 0)
    for t in range(1 + 2 * n_pairs, n):
        up(t, t % 2)
        act(t - 1, (t - 1) % 2)
    act(n - 1, (n - 1) % 2)
    o = jnp.dot(a_ref[...], wd_ref[...], preferred_element_type=F32)
    y = ALPHA * x_ref[0] + gate_ref[0] * o
    o_ref[0] = _layer_norm(y, lg_ref[...], lb_ref[...])


def _halo_specs(t, tm):
    nb = t // HALO
    per = tm // HALO
    main = pl.BlockSpec((1, tm, D_MODEL), lambda bi, i: (bi, i, 0))
    prev = pl.BlockSpec((1, HALO, D_MODEL), lambda bi, i: (bi, jnp.maximum(i * per - 1, 0), 0))
    nxt = pl.BlockSpec((1, HALO, D_MODEL), lambda bi, i: (bi, jnp.minimum((i + 1) * per, nb - 1), 0))
    return main, prev, nxt


def _const_spec(shape):
    nd = len(shape)
    return pl.BlockSpec(shape, lambda bi, i: (0,) * nd)


def _conv_ffn_ln(x, sc, sh, gate, fw, ln_g, ln_b, tm):
    b, t, _ = x.shape
    main, prev, nxt = _halo_specs(t, tm)
    vec = pl.BlockSpec((1, 1, D_MODEL), lambda bi, i: (bi, 0, 0))
    par = _const_spec((1, D_MODEL))
    wg, wv, cwg, cbg, cwv, cbv, wd = fw
    return pl.pallas_call(
        functools.partial(_conv_ffn_kernel, tm=tm),
        out_shape=jax.ShapeDtypeStruct((b, t, D_MODEL), F32),
        grid=(b, t // tm),
        in_specs=[main, prev, nxt, vec, vec, vec,
                  _const_spec(wg.shape), _const_spec(wv.shape),
                  _const_spec(cwg.shape), _const_spec(cbg.shape),
                  _const_spec(cwv.shape), _const_spec(cbv.shape),
                  _const_spec(wd.shape), par, par],
        out_specs=main,
        scratch_shapes=[pltpu.VMEM((tm + 2 * HALO, D_MODEL), BF16),
                        pltpu.VMEM((2, tm + 2 * HALO, FF_CHUNK), F32),
                        pltpu.VMEM((2, tm + 2 * HALO, FF_CHUNK), F32),
                        pltpu.VMEM((tm, D_FF), BF16)],
        compiler_params=_cparams(2), name="conv_ffn_ln",
    )(x, x, x, sc, sh, gate, wg, wv, cwg, cbg, cwv, cbv, wd, ln_g, ln_b)


def _prep_ffn(w_up, conv_w, conv_b, w_down):
    def chunks_cols(a):
        return jnp.transpose(a.reshape(a.shape[0], N_FF_CHUNKS, FF_CHUNK), (1, 0, 2))
    wg = chunks_cols(w_up[:, :D_FF]).astype(BF16)
    wv = chunks_cols(w_up[:, D_FF:]).astype(BF16)
    def sublane_rep(a):
        return jnp.broadcast_to(a[:, :, None, :], a.shape[:2] + (8, FF_CHUNK))
    cwg = sublane_rep(chunks_cols(conv_w[:, :D_FF]))
    cwv = sublane_rep(chunks_cols(conv_w[:, D_FF:]))
    cbg = sublane_rep(chunks_cols(conv_b[None, :D_FF]))
    cbv = sublane_rep(chunks_cols(conv_b[None, D_FF:]))
    wd = w_down.astype(BF16)
    return wg, wv, cwg, cbg, cwv, cbv, wd


def _rg_coeffs(xb, wgate_ref, bgate_ref, ap_ref, a_ref, b_ref):
    ap = ap_ref[...]
    neg = -ap
    sp = jnp.maximum(neg, 0.0) + jnp.log1p(jnp.exp(-jnp.abs(neg)))
    for n in range(N_RG_BLOCKS):
        cols = slice(n * RG_BW, (n + 1) * RG_BW)
        xn = xb[:, cols]
        z = jnp.dot(xn.astype(BF16), wgate_ref[n], preferred_element_type=F32) + bgate_ref[n]
        gate_a = jax.nn.sigmoid(z[:, :RG_BW])
        gate_x = jax.nn.sigmoid(z[:, RG_BW:])
        log_a = -RG_C * gate_a * sp[:, cols]
        a = jnp.exp(log_a)
        one_minus_a2 = -jnp.tanh(log_a) * (1.0 + a * a)
        a_ref[:, cols] = a
        b_ref[:, cols] = gate_x * xn * jnp.sqrt(one_minus_a2)


def _scan_tile(a_ref, b_ref, r_ref, h_in, tt, reverse):
    n_groups = tt // 8
    row = lax.broadcasted_iota(jnp.int32, (8, D_MODEL), 0)

    def body(gi, h):
        g = (n_groups - 1 - gi) if reverse else gi
        off = pl.multiple_of(g * 8, 8)
        a = a_ref[pl.ds(off, 8), :]
        b = b_ref[pl.ds(off, 8), :]
        for k in (1, 2, 4):
            shift = (8 - k) if reverse else k
            msk = (row < 8 - k) if reverse else (row >= k)
            a_sh = pltpu.roll(a, shift, 0)
            b_sh = pltpu.roll(b, shift, 0)
            b = jnp.where(msk, a * b_sh + b, b)
            a = jnp.where(msk, a * a_sh, a)
        hg = a * h + b
        r_ref[pl.ds(off, 8), :] = hg
        return hg[0:1] if reverse else hg[7:8]

    return lax.fori_loop(0, n_groups, body, h_in)


def _rnn_in_kernel(x_ref, xp_ref, xn_ref, sc_ref, sh_ref, w_ref, cw_ref, cb_ref,
                   wgate_ref, bgate_ref, ap_ref, h0_ref,
                   r_ref, y_ref, xb_ref, h_ref, u_ref, a_ref, b_ref, carry_ref, *, tt, reverse):
    i = pl.program_id(1)

    @pl.when(i == 0)
    def _():
        carry_ref[...] = h0_ref[0]

    _fill_modulated_dir(h_ref, x_ref, xp_ref, xn_ref, sc_ref, sh_ref, tt, reverse)
    h = h_ref[...]
    y_ref[0] = jnp.dot(h[HALO:HALO + tt], w_ref[:, :D_MODEL], preferred_element_type=F32).astype(BF16)
    u_ref[...] = jnp.dot(h, w_ref[:, D_MODEL:], preferred_element_type=F32)
    cw = cw_ref[...]
    xb = (cw[0:1] * u_ref[pl.ds(HALO - 2, tt), :] + cw[1:2] * u_ref[pl.ds(HALO - 1, tt), :]
          + cw[2:3] * u_ref[pl.ds(HALO, tt), :] + cw[3:4] * u_ref[pl.ds(HALO + 1, tt), :]
          + cb_ref[...])
    xb_ref[0] = xb
    _rg_coeffs(xb, wgate_ref, bgate_ref, ap_ref, a_ref, b_ref)
    carry_ref[...] = _scan_tile(a_ref, b_ref, r_ref.at[0], carry_ref[...], tt, reverse)


def _fill_modulated_dir(h_ref, x_ref, xp_ref, xn_ref, sc_ref, sh_ref, tt, reverse):
    i = pl.program_id(1)
    last = pl.num_programs(1) - 1
    ti = (last - i) if reverse else i
    scale = 1.0 + sc_ref[0]
    shift = sh_ref[0]
    h_ref[HALO:HALO + tt] = (x_ref[0] * scale + shift).astype(BF16)
    prev = xp_ref[0] * scale + shift
    nxt = xn_ref[0] * scale + shift
    h_ref[0:HALO] = jnp.where(ti > 0, prev, 0.0).astype(BF16)
    h_ref[HALO + tt:] = jnp.where(ti < last, nxt, 0.0).astype(BF16)


def _rnn_in(x, sc, sh, w_in, conv_w, conv_b, wgate, bgate, ap, h0, tt, reverse):
    b, t, _ = x.shape
    nt = t // tt
    nb = t // HALO
    per = tt // HALO

    def tix(i):
        return (nt - 1 - i) if reverse else i

    main = pl.BlockSpec((1, tt, D_MODEL), lambda bi, i: (bi, tix(i), 0))
    prev = pl.BlockSpec((1, HALO, D_MODEL), lambda bi, i: (bi, jnp.maximum(tix(i) * per - 1, 0), 0))
    nxt = pl.BlockSpec((1, HALO, D_MODEL), lambda bi, i: (bi, jnp.minimum((tix(i) + 1) * per, nb - 1), 0))
    vec = pl.BlockSpec((1, 1, D_MODEL), lambda bi, i: (bi, 0, 0))
    return pl.pallas_call(
        functools.partial(_rnn_in_kernel, tt=tt, reverse=reverse),
        out_shape=(jax.ShapeDtypeStruct((b, t, D_MODEL), F32),
                   jax.ShapeDtypeStruct((b, t, D_MODEL), BF16),
                   jax.ShapeDtypeStruct((b, t, D_MODEL), F32)),
        grid=(b, nt),
        in_specs=[main, prev, nxt, vec, vec,
                  _const_spec(w_in.shape), _const_spec(conv_w.shape), _const_spec((1, D_MODEL)),
                  _const_spec(wgate.shape), _const_spec(bgate.shape), _const_spec((1, D_MODEL)), vec],
        out_specs=(main, main, main),
        scratch_shapes=[pltpu.VMEM((tt + 2 * HALO, D_MODEL), BF16),
                        pltpu.VMEM((tt + 2 * HALO, D_MODEL), F32),
                        pltpu.VMEM((tt, D_MODEL), F32),
                        pltpu.VMEM((tt, D_MODEL), F32),
                        pltpu.VMEM((1, D_MODEL), F32)],
        compiler_params=_cparams(2), name="rnn_in_rev" if reverse else "rnn_in_fwd",
    )(x, x, x, sc, sh, w_in, conv_w, conv_b, wgate, bgate, ap, h0)


def _rnn_out_kernel(xb_ref, y_ref, rf_ref, x_ref, gate_ref, wgate_ref, bgate_ref, ap_ref, h0_ref,
                    w_ref, lg_ref, lb_ref, o_ref, a_ref, b_ref, rb_ref, carry_ref, *, tt):
    i = pl.program_id(1)

    @pl.when(i == 0)
    def _():
        carry_ref[...] = h0_ref[0]

    _rg_coeffs(xb_ref[0], wgate_ref, bgate_ref, ap_ref, a_ref, b_ref)
    carry_ref[...] = _scan_tile(a_ref, b_ref, rb_ref, carry_ref[...], tt, True)
    r = rf_ref[0] + rb_ref[...]
    m = (r * jax.nn.gelu(y_ref[0].astype(F32), approximate=True)).astype(BF16)
    o = jnp.dot(m, w_ref[...], preferred_element_type=F32)
    y = ALPHA * x_ref[0] + gate_ref[0] * o
    o_ref[0] = _layer_norm(y, lg_ref[...], lb_ref[...])


def _rnn_out(xb, y, rf, x, gate, wgate, bgate, ap, h0, w_out, ln_g, ln_b, tt):
    b, t, _ = x.shape
    nt = t // tt
    main = pl.BlockSpec((1, tt, D_MODEL), lambda bi, i: (bi, nt - 1 - i, 0))
    vec = pl.BlockSpec((1, 1, D_MODEL), lambda bi, i: (bi, 0, 0))
    par = _const_spec((1, D_MODEL))
    return pl.pallas_call(
        functools.partial(_rnn_out_kernel, tt=tt),
        out_shape=jax.ShapeDtypeStruct((b, t, D_MODEL), F32),
        grid=(b, nt),
        in_specs=[main, main, main, main, vec,
                  _const_spec(wgate.shape), _const_spec(bgate.shape), par, vec,
                  _const_spec(w_out.shape), par, par],
        out_specs=main,
        scratch_shapes=[pltpu.VMEM((tt, D_MODEL), F32)] * 3 + [pltpu.VMEM((1, D_MODEL), F32)],
        compiler_params=_cparams(2), name="rnn_out",
    )(xb, y, rf, x, gate, wgate, bgate, ap, h0, w_out, ln_g, ln_b)


def _rope_tables(n_tok):
    t = jnp.arange(n_tok)
    row = (t // GRID_W).astype(F32)[:, None]
    col = (t % GRID_W).astype(F32)[:, None]
    nf = HD // 4
    inv = 1.0 / (ROPE_THETA ** (jnp.arange(nf, dtype=F32) / nf))
    ang = jnp.concatenate([row * inv, row * inv, col * inv, col * inv], -1)
    sign = jnp.where((jnp.arange(HD) % (HD // 2)) < nf, -1.0, 1.0)
    reps = D_A // HD
    return jnp.tile(jnp.cos(ang), (1, reps)), jnp.tile(jnp.sin(ang) * sign, (1, reps))


def _tile(n, pref):
    return pref if n % pref == 0 else n


def kernel(x, c, ctx, c_ctx, ada_w, ada_b, ln1_g, ln1_b, ln2_g, ln2_b, ffn_w_up, ffn_conv_w, ffn_conv_b, ffn_w_down, att_w_in, att_w_out, diff_lq1, diff_lk1, diff_lq2, diff_lk2, diff_subln_g, na_rpb, rnn_w_in, rnn_conv_w, rnn_conv_b, rg_a_param, rg_wa, rg_ba, rg_wx, rg_bx, rnn_w_out):
    bsz, t, _ = x.shape
    n_ctx = ctx.shape[1]
    assert t % GRID_W == 0 and t // GRID_W >= WIN_R and n_ctx % HALO == 0

    n_rows = ((bsz + 1 + 7) // 8) * 8
    cc = jnp.zeros((n_rows, D_MODEL), F32).at[:bsz].set(c).at[bsz].set(c_ctx)
    mod = _ada_mod(cc, ada_w, ada_b)

    def mods(l):
        lat = mod[l, :bsz].reshape(bsz, 1, 6, D_MODEL)
        cx = jnp.broadcast_to(mod[l, bsz].reshape(1, 1, 6, D_MODEL), (bsz, 1, 6, D_MODEL))
        return ([lat[:, :, k] for k in range(6)], [cx[:, :, k] for k in range(6)])

    tm = _tile(t, 512)
    tmc = _tile(n_ctx, 512)

    (sh1, sc1, g1, sh2, sc2, g2), (csh1, csc1, cg1, csh2, csc2, cg2) = mods(0)
    lambda_init = 0.8 - 0.6 * math.exp(-0.3 * 0)
    w_in = att_w_in[0].astype(BF16)
    w_out = att_w_out[0].astype(BF16)
    cos_t, sin_t = _rope_tables(t)
    qa, ka, va, qb, kb, vb = _proj_attn(x, sc1, sh1, w_in, cos_t, sin_t, True, tm)
    qa_c, ka_c, va_c, qb_c, kb_c, vb_c = _proj_attn(
        ctx, csc1, csh1, w_in, cos_t[:n_ctx], sin_t[:n_ctx], False, tmc)
    lp = jnp.stack([diff_lq1[0], diff_lk1[0], diff_lq2[0], diff_lk2[0]])
    sub_g = diff_subln_g[0].reshape(1, 2 * HD)
    oa = _diff_attn(lp, sub_g, qa, ka_c, va_c, ka, va, lambda_init, _tile(t, 256), _tile(t, 512))
    ob = _na_attn(qb, kb, vb, kb_c, vb_c, _na_bias_table(na_rpb[0]))
    oa_c, ob_c = _ctx_attn(lp, sub_g, qa_c, ka_c, va_c, qb_c, kb_c, vb_c, lambda_init)
    l1g, l1b = ln1_g[0].reshape(1, D_MODEL), ln1_b[0].reshape(1, D_MODEL)
    l2g, l2b = ln2_g[0].reshape(1, D_MODEL), ln2_b[0].reshape(1, D_MODEL)
    x = _outproj_ln(oa, ob, w_out, x, g1, l1g, l1b, tm)
    ctx = _outproj_ln(oa_c, ob_c, w_out, ctx, cg1, l1g, l1b, tmc)
    fw = _prep_ffn(ffn_w_up[0], ffn_conv_w[0], ffn_conv_b[0], ffn_w_down[0])
    x = _conv_ffn_ln(x, sc2, sh2, g2, fw, l2g, l2b, tm)
    ctx = _conv_ffn_ln(ctx, csc2, csh2, cg2, fw, l2g, l2b, tmc)

    (sh1, sc1, g1, sh2, sc2, g2), (csh1, csc1, _, _, _, _) = mods(1)
    w_in = rnn_w_in[0].astype(BF16)
    w_out = rnn_w_out[0].astype(BF16)
    conv_b = rnn_conv_b[0].reshape(1, D_MODEL)
    wgate = [jnp.concatenate([rg_wa[0, d], rg_wx[0, d]], axis=-1).astype(BF16) for d in range(2)]
    bgate = [jnp.concatenate([rg_ba[0, d], rg_bx[0, d]], axis=-1).reshape(N_RG_BLOCKS, 1, 2 * RG_BW)
             for d in range(2)]
    ap = [rg_a_param[0, d].reshape(1, D_MODEL) for d in range(2)]
    zeros = jnp.zeros((bsz, 1, D_MODEL), F32)
    tt = _tile(t, 256)
    ttc = _tile(n_ctx, 256)
    rc_f, _, _ = _rnn_in(ctx, csc1, csh1, w_in, rnn_conv_w[0], conv_b, wgate[0], bgate[0], ap[0],
                         zeros, ttc, False)
    rc_b, _, _ = _rnn_in(ctx, csc1, csh1, w_in, rnn_conv_w[0], conv_b, wgate[1], bgate[1], ap[1],
                         zeros, ttc, True)
    h0_f = rc_f[:, n_ctx - 1:n_ctx]
    h0_b = rc_b[:, 0:1]
    r_f, y, xb = _rnn_in(x, sc1, sh1, w_in, rnn_conv_w[0], conv_b, wgate[0], bgate[0], ap[0],
                         h0_f, tt, False)
    l1g, l1b = ln1_g[1].reshape(1, D_MODEL), ln1_b[1].reshape(1, D_MODEL)
    l2g, l2b = ln2_g[1].reshape(1, D_MODEL), ln2_b[1].reshape(1, D_MODEL)
    x = _rnn_out(xb, y, r_f, x, g1, wgate[1], bgate[1], ap[1], h0_b, w_out, l1g, l1b, tt)
    fw = _prep_ffn(ffn_w_up[1], ffn_conv_w[1], ffn_conv_b[1], ffn_w_down[1])
    x = _conv_ffn_ln(x, sc2, sh2, g2, fw, l2g, l2b, tm)
    return x
```

```python
import functools
import math

import jax
import jax.numpy as jnp
from jax import lax
from jax.experimental import pallas as pl
from jax.experimental.pallas import tpu as pltpu

F32 = jnp.float32
BF16 = jnp.bfloat16

D_MODEL = 1024
DEPTH = 2
GRID_W = 64
HD = 64
D_A = D_MODEL // 2
D_B = D_MODEL - D_A
H_A = D_A // (2 * HD)
H_B = D_B // HD
WIN_R = 8
WIN_C = 16
ROPE_THETA = 10000.0
N_RG_BLOCKS = 8
RG_BW = D_MODEL // N_RG_BLOCKS
RG_C = 8.0
CONV_RNN = 4
D_FF = ((8 * D_MODEL // 3 + 127) // 128) * 128
CONV_FFN = 3
ALPHA = (2 * DEPTH) ** 0.25
NORM_EPS = 1e-5
NEG_BIG = -1e30
Q_SCALE = HD ** -0.5 * math.log2(math.e)

LANES = 128
HALO = 16
DENOM_ROWS = 16
FF_CHUNK = 256
N_FF_CHUNKS = D_FF // FF_CHUNK
VMEM_LIMIT = 56 * 1024 * 1024

_NT = (((1,), (1,)), ((), ()))


def _cparams(n_axes, flags=None):
    return pltpu.CompilerParams(dimension_semantics=("arbitrary",) * n_axes,
                                vmem_limit_bytes=VMEM_LIMIT, flags=flags)


def _layer_norm(y, g, b):
    mu = jnp.mean(y, axis=-1, keepdims=True)
    yc = y - mu
    var = jnp.mean(yc * yc, axis=-1, keepdims=True)
    return yc * lax.rsqrt(var + NORM_EPS) * g + b


def _softmax_parts(s_list):
    m = s_list[0].max(axis=-1, keepdims=True)
    for s in s_list[1:]:
        m = jnp.maximum(m, s.max(axis=-1, keepdims=True))
    ps = [jnp.exp2(s - m) for s in s_list]
    l = ps[0].sum(axis=-1, keepdims=True)
    for p in ps[1:]:
        l = l + p.sum(axis=-1, keepdims=True)
    return ps, 1.0 / l


def _half_masks(dtype):
    lane = lax.broadcasted_iota(jnp.int32, (1, LANES), 1)
    lo = (lane < HD).astype(dtype)
    return lo, (1 - lo).astype(dtype)


def _ada_kernel(c_ref, w_ref, b_ref, o_ref):
    c = c_ref[...]
    a = c * jax.nn.sigmoid(c)
    o_ref[0] = jnp.dot(a, w_ref[0], preferred_element_type=F32,
                       precision=lax.Precision.HIGHEST) + b_ref[0]


def _ada_mod(cc, ada_w, ada_b):
    n = cc.shape[0]
    tn = 1536
    return pl.pallas_call(
        _ada_kernel,
        out_shape=jax.ShapeDtypeStruct((DEPTH, n, 6 * D_MODEL), F32),
        grid=(DEPTH, 6 * D_MODEL // tn),
        in_specs=[pl.BlockSpec((n, D_MODEL), lambda l, j: (0, 0)),
                  pl.BlockSpec((1, D_MODEL, tn), lambda l, j: (l, 0, j)),
                  pl.BlockSpec((1, 1, tn), lambda l, j: (l, 0, j))],
        out_specs=pl.BlockSpec((1, n, tn), lambda l, j: (l, 0, j)),
        compiler_params=_cparams(2), name="ada_mod",
    )(cc, ada_w, ada_b.reshape(DEPTH, 1, 6 * D_MODEL))


def _proj_attn_kernel(x_ref, sc_ref, sh_ref, w_ref, cos_ref, sin_ref,
                      qa_ref, ka_ref, va_ref, qb_ref, kb_ref, vb_ref, vt_ref, *, rope):
    h = (x_ref[0] * (1.0 + sc_ref[0]) + sh_ref[0]).astype(BF16)
    outs = (qa_ref, ka_ref, va_ref, qb_ref, kb_ref, vb_ref)
    for j, o_ref in enumerate(outs):
        r = jnp.dot(h, w_ref[:, j * D_A:(j + 1) * D_A], preferred_element_type=F32)
        if rope and j < 2:
            lane = lax.broadcasted_iota(jnp.int32, r.shape, 1)
            first_half = (lane % (HD // 2)) < (HD // 4)
            rot = jnp.where(first_half, pltpu.roll(r, D_A - HD // 4, 1), pltpu.roll(r, HD // 4, 1))
            r = r * cos_ref[...] + rot * sin_ref[...]
        if j in (0, 3):
            r = r * Q_SCALE
        if j in (2, 5):
            vt_ref[...] = r
            o_ref[0] = vt_ref[...].T.astype(BF16)
        else:
            o_ref[0] = r.astype(BF16)


def _proj_attn(x, sc, sh, w, cos_t, sin_t, rope, tm):
    b, t, _ = x.shape
    out = jax.ShapeDtypeStruct((b, t, D_A), BF16)
    out_t = jax.ShapeDtypeStruct((b, D_A, t), BF16)
    vec = pl.BlockSpec((1, 1, D_MODEL), lambda bi, i: (bi, 0, 0))
    blk = pl.BlockSpec((1, tm, D_A), lambda bi, i: (bi, i, 0))
    blk_t = pl.BlockSpec((1, D_A, tm), lambda bi, i: (bi, 0, i))
    tab = pl.BlockSpec((tm, D_A), lambda bi, i: (i, 0))
    return pl.pallas_call(
        functools.partial(_proj_attn_kernel, rope=rope),
        out_shape=(out, out, out_t, out, out, out_t),
        grid=(b, t // tm),
        in_specs=[pl.BlockSpec((1, tm, D_MODEL), lambda bi, i: (bi, i, 0)), vec, vec,
                  pl.BlockSpec((D_MODEL, 6 * D_A), lambda bi, i: (0, 0)), tab, tab],
        out_specs=(blk, blk, blk_t, blk, blk, blk_t),
        scratch_shapes=[pltpu.VMEM((tm, D_A), F32)],
        compiler_params=_cparams(2), name="proj_attn",
    )(x, sc, sh, w, cos_t, sin_t)


def _lambda_full(lp_ref, lambda_init):
    lp = lp_ref[...]
    s1 = jnp.sum(lp[0:1] * lp[1:2], axis=-1, keepdims=True)
    s2 = jnp.sum(lp[2:3] * lp[3:4], axis=-1, keepdims=True)
    return jnp.exp(s1) - jnp.exp(s2) + lambda_init


def _head_rms(o, g, lambda_init):
    o = o * lax.rsqrt(jnp.mean(o * o, axis=-1, keepdims=True) + NORM_EPS)
    return o * g * (1.0 - lambda_init)


def _diff_attn_kernel(lp_ref, g_ref, q_ref, kc_ref, vct_ref, kl_ref, vlt_ref, o_ref,
                      s_ref, p_ref, acc_ref, *, tk, n_chunks, n_ctx, lambda_init):
    lam = _lambda_full(lp_ref, lambda_init)
    q = q_ref[0]
    lo, hi = _half_masks(BF16)
    qs = (q * lo, q * hi)
    tq = q.shape[0]
    chunks = [(kc_ref, vct_ref, 0, n_ctx)] + [(kl_ref, vlt_ref, j * tk, tk) for j in range(n_chunks)]
    n_all = len(chunks)
    acc_ref[...] = jnp.zeros_like(acc_ref)

    def scores(c):
        k_ref, _, off, size = chunks[c]
        k = k_ref[0, off:off + size, :]
        for mi in range(2):
            s_ref[c % 2, mi, :size, :] = lax.dot_general(k, qs[mi], _NT, preferred_element_type=F32)

    def softmax(c, m_old):
        size = chunks[c][3]
        m_out, a_out = [], []
        for mi in range(2):
            s = s_ref[c % 2, mi, :size, :]
            m_new = jnp.maximum(m_old[mi], s.max(axis=0, keepdims=True))
            a_out.append(jnp.exp2(m_old[mi] - m_new))
            p_ref[c % 2, mi, :size, :] = jnp.exp2(s - m_new).astype(BF16)
            m_out.append(m_new)
        return m_out, a_out

    def values(c, a):
        _, vt_ref, off, size = chunks[c]
        vext = jnp.concatenate([vt_ref[0, :, off:off + size], jnp.ones((DENOM_ROWS, size), BF16)], axis=0)
        for mi in range(2):
            pv = jnp.dot(vext, p_ref[c % 2, mi, :size, :], preferred_element_type=F32)
            acc_ref[mi] = a[mi] * acc_ref[mi] + pv

    m = [jnp.full((1, tq), NEG_BIG, F32)] * 2
    a_next = None
    for i in range(-2, n_all):
        a_cur = a_next
        if i + 2 < n_all:
            scores(i + 2)
        if 0 <= i + 1 < n_all:
            m, a_next = softmax(i + 1, m)
        if i >= 0:
            values(i, a_cur)

    outs = []
    for mi in range(2):
        acc = acc_ref[mi]
        outs.append(acc[:2 * HD] / acc[2 * HD:2 * HD + 1])
    o = (outs[0] - lam * outs[1]).T
    o_ref[0] = _head_rms(o, g_ref[...], lambda_init).astype(BF16)


def _diff_attn(lp, g, qa, ka_c, vat_c, ka, vat, lambda_init, tq, tk):
    b, t, _ = qa.shape
    n_ctx = ka_c.shape[1]
    assert n_ctx <= tk and n_ctx % LANES == 0
    q_blk = pl.BlockSpec((1, tq, LANES), lambda bi, h, i: (bi, i, h))
    return pl.pallas_call(
        functools.partial(_diff_attn_kernel, tk=tk, n_chunks=t // tk, n_ctx=n_ctx, lambda_init=lambda_init),
        out_shape=jax.ShapeDtypeStruct((b, t, D_A), BF16),
        grid=(b, H_A, t // tq),
        in_specs=[pl.BlockSpec((4, HD), lambda bi, h, i: (0, 0)),
                  pl.BlockSpec((1, 2 * HD), lambda bi, h, i: (0, 0)),
                  q_blk,
                  pl.BlockSpec((1, n_ctx, LANES), lambda bi, h, i: (bi, 0, h)),
                  pl.BlockSpec((1, LANES, n_ctx), lambda bi, h, i: (bi, h, 0)),
                  pl.BlockSpec((1, t, LANES), lambda bi, h, i: (bi, 0, h)),
                  pl.BlockSpec((1, LANES, t), lambda bi, h, i: (bi, h, 0))],
        out_specs=q_blk,
        scratch_shapes=[pltpu.VMEM((2, 2, tk, tq), F32),
                        pltpu.VMEM((2, 2, tk, tq), BF16),
                        pltpu.VMEM((2, 2 * HD + DENOM_ROWS, tq), F32)],
        compiler_params=_cparams(3), name="diff_attn",
    )(lp, g, qa, ka_c, vat_c, ka, vat)


NA_QROWS = 4
NA_KROWS = NA_QROWS + WIN_R
NA_REL_PAD = NA_QROWS
NA_REL_N = NA_KROWS + WIN_R + NA_QROWS - 2


def _na_kernel(q_ref, k_ref, vt_ref, kc_ref, vct_ref, tab_ref, o_ref, s_ref, p_ref, ot_ref, *, rows, n_ctx):
    g = pl.program_id(1)
    r0 = g * NA_QROWS
    ws = jnp.clip(r0 - WIN_R // 2, 0, rows - NA_KROWS)
    d0 = r0 - ws
    k_start = pl.multiple_of(ws * GRID_W, NA_QROWS * GRID_W)
    n_loc = NA_KROWS * GRID_W
    nq = NA_QROWS * GRID_W
    lo, hi = _half_masks(BF16)

    qr = lax.broadcasted_iota(jnp.int32, (1, nq), 1) // GRID_W
    rs = jnp.clip(r0 + qr - WIN_R // 2, 0, rows - WIN_R)
    row_mask = []
    for kr in range(NA_KROWS):
        valid = (ws + kr >= rs) & (ws + kr < rs + WIN_R)
        row_mask.append(jnp.where(valid, 0.0, NEG_BIG))

    def scores(h):
        hp, e = divmod(h, 2)
        cols = slice(hp * LANES, (hp + 1) * LANES)
        qm = q_ref[0, :, cols] * (lo, hi)[e]
        s_loc = lax.dot_general(k_ref[0, pl.ds(k_start, n_loc), cols], qm, _NT, preferred_element_type=F32)
        for kr in range(NA_KROWS):
            base = kr - d0 + (WIN_R - 1) + (NA_REL_PAD - 1)
            bias = jnp.concatenate([tab_ref[h, base - 2 * p] for p in range(NA_QROWS // 2)], axis=1)
            rows_kr = slice(kr * GRID_W, (kr + 1) * GRID_W)
            s_ref[h % 2, rows_kr, :] = s_loc[rows_kr] + bias + row_mask[kr]
        s_ref[h % 2, n_loc:, :] = lax.dot_general(kc_ref[0, :, cols], qm, _NT, preferred_element_type=F32)

    def softmax(h):
        s = s_ref[h % 2]
        m = s.max(axis=0, keepdims=True)
        p_ref[h % 2] = jnp.exp2(s - m).astype(BF16)

    def values(h):
        hrows = slice(h * HD, (h + 1) * HD)
        vt = jnp.concatenate([vt_ref[0, hrows, pl.ds(k_start, n_loc)], vct_ref[0, hrows, :]], axis=1)
        vext = jnp.concatenate([vt, jnp.ones((DENOM_ROWS, n_loc + n_ctx), BF16)], axis=0)
        o = jnp.dot(vext, p_ref[h % 2], preferred_element_type=F32)
        ot_ref[hrows, :] = o[:HD] / o[HD:HD + 1]

    for i in range(-2, H_B):
        if i + 2 < H_B:
            scores(i + 2)
        if 0 <= i + 1 < H_B:
            softmax(i + 1)
        if i >= 0:
            values(i)
    o_ref[0] = ot_ref[...].T.astype(BF16)


def _na_bias_table(rpb):
    c = jnp.arange(GRID_W)[None, :]
    kc = jnp.arange(GRID_W)[:, None]
    cstart = jnp.clip(c - WIN_C // 2, 0, GRID_W - WIN_C)
    inside = (kc >= cstart) & (kc < cstart + WIN_C)
    rel_c = jnp.clip(kc - c + (WIN_C - 1), 0, 2 * WIN_C - 2)
    band = jnp.where(inside, rpb[:, :, rel_c] * math.log2(math.e), NEG_BIG)
    hi_pad = NA_REL_N - NA_REL_PAD + 1 - (2 * WIN_R - 1)
    band = jnp.pad(band, ((0, 0), (NA_REL_PAD, hi_pad), (0, 0), (0, 0)), constant_values=NEG_BIG)
    return jnp.concatenate([band[:, 1:], band[:, :-1]], axis=-1).astype(F32)


def _na_attn(qb, kb, vbt, kb_c, vbt_c, tab):
    b, t, _ = qb.shape
    rows = t // GRID_W
    n_ctx = kb_c.shape[1]
    assert rows % NA_QROWS == 0 and rows >= NA_KROWS and tab.shape[1] == NA_REL_N
    nq = NA_QROWS * GRID_W
    n_keys = NA_KROWS * GRID_W + n_ctx
    qblk = pl.BlockSpec((1, nq, D_B), lambda bi, g: (bi, g, 0))
    return pl.pallas_call(
        functools.partial(_na_kernel, rows=rows, n_ctx=n_ctx),
        out_shape=jax.ShapeDtypeStruct((b, t, D_B), BF16),
        grid=(b, rows // NA_QROWS),
        in_specs=[qblk,
                  pl.BlockSpec((1, t, D_B), lambda bi, g: (bi, 0, 0)),
                  pl.BlockSpec((1, D_B, t), lambda bi, g: (bi, 0, 0)),
                  pl.BlockSpec((1, n_ctx, D_B), lambda bi, g: (bi, 0, 0)),
                  pl.BlockSpec((1, D_B, n_ctx), lambda bi, g: (bi, 0, 0)),
                  pl.BlockSpec(tab.shape, lambda bi, g: (0, 0, 0, 0))],
        out_specs=qblk,
        scratch_shapes=[pltpu.VMEM((2, n_keys, nq), F32),
                        pltpu.VMEM((2, n_keys, nq), BF16),
                        pltpu.VMEM((D_B, nq), F32)],
        compiler_params=_cparams(2), name="na_attn",
    )(qb, kb, vbt, kb_c, vbt_c, tab)


def _ctx_attn_kernel(lp_ref, g_ref, qa_ref, ka_ref, vat_ref, qb_ref, kb_ref, vbt_ref,
                     oa_ref, ob_ref, *, lambda_init):
    lam = _lambda_full(lp_ref, lambda_init)
    lo, hi = _half_masks(BF16)
    lane = lax.broadcasted_iota(jnp.int32, (1, LANES), 1)
    for h in range(H_A):
        cols = slice(h * LANES, (h + 1) * LANES)
        q, k, vt = qa_ref[0, :, cols], ka_ref[0, :, cols], vat_ref[0, cols, :]
        probs = []
        for msk in (lo, hi):
            s = lax.dot_general(q * msk, k, _NT, preferred_element_type=F32)
            (p,), inv = _softmax_parts([s])
            probs.append(p * inv)
        w = (probs[0] - lam * probs[1]).astype(BF16)
        o = lax.dot_general(w, vt, _NT, preferred_element_type=F32)
        oa_ref[0, :, cols] = _head_rms(o, g_ref[...], lambda_init).astype(BF16)
    for hp in range(H_B // 2):
        cols = slice(hp * LANES, (hp + 1) * LANES)
        q, k, vt = qb_ref[0, :, cols], kb_ref[0, :, cols], vbt_ref[0, cols, :]
        halves = []
        for msk in (lo, hi):
            s = lax.dot_general(q * msk, k, _NT, preferred_element_type=F32)
            (p,), inv = _softmax_parts([s])
            halves.append(lax.dot_general((p * inv).astype(BF16), vt, _NT, preferred_element_type=F32))
        ob_ref[0, :, cols] = jnp.where(lane < HD, halves[0], halves[1]).astype(BF16)


def _ctx_attn(lp, g, qa, ka, vat, qb, kb, vbt, lambda_init):
    b, n, _ = qa.shape
    blk = pl.BlockSpec((1, n, D_A), lambda bi: (bi, 0, 0))
    blk_t = pl.BlockSpec((1, D_A, n), lambda bi: (bi, 0, 0))
    out = jax.ShapeDtypeStruct((b, n, D_A), BF16)
    return pl.pallas_call(
        functools.partial(_ctx_attn_kernel, lambda_init=lambda_init),
        out_shape=(out, out),
        grid=(b,),
        in_specs=[pl.BlockSpec((4, HD), lambda bi: (0, 0)),
                  pl.BlockSpec((1, 2 * HD), lambda bi: (0, 0)),
                  blk, blk, blk_t, blk, blk, blk_t],
        out_specs=(blk, blk),
        compiler_params=_cparams(1), name="ctx_attn",
    )(lp, g, qa, ka, vat, qb, kb, vbt)


def _outproj_ln_kernel(oa_ref, ob_ref, w_ref, x_ref, gate_ref, g_ref, b_ref, o_ref):
    o = (jnp.dot(oa_ref[0], w_ref[:D_A], preferred_element_type=F32)
         + jnp.dot(ob_ref[0], w_ref[D_A:], preferred_element_type=F32))
    y = ALPHA * x_ref[0] + gate_ref[0] * o
    o_ref[0] = _layer_norm(y, g_ref[...], b_ref[...])


def _outproj_ln(oa, ob, w, x, gate, ln_g, ln_b, tm):
    b, t, _ = x.shape
    half = pl.BlockSpec((1, tm, D_A), lambda bi, i: (bi, i, 0))
    full = pl.BlockSpec((1, tm, D_MODEL), lambda bi, i: (bi, i, 0))
    vec = pl.BlockSpec((1, 1, D_MODEL), lambda bi, i: (bi, 0, 0))
    par = pl.BlockSpec((1, D_MODEL), lambda bi, i: (0, 0))
    return pl.pallas_call(
        _outproj_ln_kernel,
        out_shape=jax.ShapeDtypeStruct((b, t, D_MODEL), F32),
        grid=(b, t // tm),
        in_specs=[half, half, pl.BlockSpec((D_MODEL, D_MODEL), lambda bi, i: (0, 0)),
                  full, vec, par, par],
        out_specs=full,
        compiler_params=_cparams(2), name="outproj_ln",
    )(oa, ob, w, x, gate, ln_g, ln_b)


def _fill_modulated(h_ref, x_ref, xp_ref, xn_ref, sc_ref, sh_ref, tm):
    i = pl.program_id(1)
    last = pl.num_programs(1) - 1
    scale = 1.0 + sc_ref[0]
    shift = sh_ref[0]
    h_ref[HALO:HALO + tm] = (x_ref[0] * scale + shift).astype(BF16)
    prev = xp_ref[0] * scale + shift
    nxt = xn_ref[0] * scale + shift
    h_ref[0:HALO] = jnp.where(i > 0, prev, 0.0).astype(BF16)
    h_ref[HALO + tm:] = jnp.where(i < last, nxt, 0.0).astype(BF16)


def _conv_ffn_kernel(x_ref, xp_ref, xn_ref, sc_ref, sh_ref, gate_ref,
                     wg_ref, wv_ref, cwg_ref, cbg_ref, cwv_ref, cbv_ref, wd_ref,
                     lg_ref, lb_ref, o_ref, h_ref, ug_ref, uv_ref, a_ref, *, tm):
    _fill_modulated(h_ref, x_ref, xp_ref, xn_ref, sc_ref, sh_ref, tm)

    def conv3(u_ref, slot, cw, cb):
        u = u_ref[slot]
        rows, cols = u.shape
        u3 = u.reshape(rows // 8, 8, cols)
        sub = lax.broadcasted_iota(jnp.int32, u3.shape, 1)
        dn = pltpu.roll(u3, 1, 1)
        dn = jnp.where(sub == 0, jnp.concatenate([dn[-1:], dn[:-1]], axis=0), dn)
        up_ = pltpu.roll(u3, 7, 1)
        up_ = jnp.where(sub == 7, jnp.concatenate([up_[1:], up_[:1]], axis=0), up_)
        full = cw[0][None] * dn + cw[1][None] * u3 + cw[2][None] * up_ + cb[0][None]
        return full.reshape(rows, cols)[HALO:HALO + tm]

    def up(c, slot):
        h = h_ref[...]
        ug_ref[slot] = jnp.dot(h, wg_ref[c], preferred_element_type=F32)
        uv_ref[slot] = jnp.dot(h, wv_ref[c], preferred_element_type=F32)

    def act(c, slot):
        g = conv3(ug_ref, slot, cwg_ref[c], cbg_ref[c])
        v = conv3(uv_ref, slot, cwv_ref[c], cbv_ref[c])
        col = c * FF_CHUNK
        if not isinstance(c, int):
            col = pl.multiple_of(col, FF_CHUNK)
        a_ref[:, pl.ds(col, FF_CHUNK)] = (jax.nn.gelu(g, approximate=True) * v).astype(BF16)

    n = N_FF_CHUNKS
    up(0, 0)
    n_pairs = (n - 1) // 2

    def body(j, _):
        t = 1 + 2 * j
        up(t, 1)
        act(t - 1, 0)
        up(t + 1, 0)
        act(t, 1)
        return 0

    lax.fori_loop(0, n_pairs, body, 0)
    for t in range(1 + 2 * n_pairs, n):
        up(t, t % 2)
        act(t - 1, (t - 1) % 2)
    act(n - 1, (n - 1) % 2)
    o = jnp.dot(a_ref[...], wd_ref[...], preferred_element_type=F32)
    y = ALPHA * x_ref[0] + gate_ref[0] * o
    o_ref[0] = _layer_norm(y, lg_ref[...], lb_ref[...])


def _halo_specs(t, tm):
    nb = t // HALO
    per = tm // HALO
    main = pl.BlockSpec((1, tm, D_MODEL), lambda bi, i: (bi, i, 0))
    prev = pl.BlockSpec((1, HALO, D_MODEL), lambda bi, i: (bi, jnp.maximum(i * per - 1, 0), 0))
    nxt = pl.BlockSpec((1, HALO, D_MODEL), lambda bi, i: (bi, jnp.minimum((i + 1) * per, nb - 1), 0))
    return main, prev, nxt


def _const_spec(shape):
    nd = len(shape)
    return pl.BlockSpec(shape, lambda bi, i: (0,) * nd, pipeline_mode=pl.Buffered(1))


def _conv_ffn_ln(x, sc, sh, gate, fw, ln_g, ln_b, tm):
    b, t, _ = x.shape
    main, prev, nxt = _halo_specs(t, tm)
    vec = pl.BlockSpec((1, 1, D_MODEL), lambda bi, i: (bi, 0, 0))
    par = _const_spec((1, D_MODEL))
    wg, wv, cwg, cbg, cwv, cbv, wd = fw
    return pl.pallas_call(
        functools.partial(_conv_ffn_kernel, tm=tm),
        out_shape=jax.ShapeDtypeStruct((b, t, D_MODEL), F32),
        grid=(b, t // tm),
        in_specs=[main, prev, nxt, vec, vec, vec,
                  _const_spec(wg.shape), _const_spec(wv.shape),
                  _const_spec(cwg.shape), _const_spec(cbg.shape),
                  _const_spec(cwv.shape), _const_spec(cbv.shape),
                  _const_spec(wd.shape), par, par],
        out_specs=main,
        scratch_shapes=[pltpu.VMEM((tm + 2 * HALO, D_MODEL), BF16),
                        pltpu.VMEM((2, tm + 2 * HALO, FF_CHUNK), F32),
                        pltpu.VMEM((2, tm + 2 * HALO, FF_CHUNK), F32),
                        pltpu.VMEM((tm, D_FF), BF16)],
        compiler_params=_cparams(2), name="conv_ffn_ln",
    )(x, x, x, sc, sh, gate, wg, wv, cwg, cbg, cwv, cbv, wd, ln_g, ln_b)


def _prep_ffn(w_up, conv_w, conv_b, w_down):
    def chunks_cols(a):
        return jnp.transpose(a.reshape(a.shape[0], N_FF_CHUNKS, FF_CHUNK), (1, 0, 2))
    wg = chunks_cols(w_up[:, :D_FF]).astype(BF16)
    wv = chunks_cols(w_up[:, D_FF:]).astype(BF16)
    def sublane_rep(a):
        return jnp.broadcast_to(a[:, :, None, :], a.shape[:2] + (8, FF_CHUNK))
    cwg = sublane_rep(chunks_cols(conv_w[:, :D_FF]))
    cwv = sublane_rep(chunks_cols(conv_w[:, D_FF:]))
    cbg = sublane_rep(chunks_cols(conv_b[None, :D_FF]))
    cbv = sublane_rep(chunks_cols(conv_b[None, D_FF:]))
    wd = w_down.astype(BF16)
    return wg, wv, cwg, cbg, cwv, cbv, wd


def _rg_coeffs(xb, wgate_ref, bgate_ref, ap_ref, a_ref, b_ref):
    ap = ap_ref[...]
    neg = -ap
    sp = jnp.maximum(neg, 0.0) + jnp.log1p(jnp.exp(-jnp.abs(neg)))
    half_c = (-0.5 * RG_C) * sp
    for n in range(N_RG_BLOCKS):
        cols = slice(n * RG_BW, (n + 1) * RG_BW)
        xn = xb[:, cols]
        t = jnp.tanh(jnp.dot(xn.astype(BF16), wgate_ref[n], preferred_element_type=F32) + bgate_ref[n])
        log_a = half_c[:, cols] + half_c[:, cols] * t[:, :RG_BW]
        gate_x = 0.5 + 0.5 * t[:, RG_BW:]
        a = jnp.exp(log_a)
        one_minus_a2 = -jnp.tanh(log_a) * (1.0 + a * a)
        a_ref[:, cols] = a
        b_ref[:, cols] = gate_x * xn * jnp.sqrt(one_minus_a2)


def _scan_tile(a_ref, b_ref, r_ref, h_in, tt, reverse):
    n_groups = tt // 8
    row = lax.broadcasted_iota(jnp.int32, (8, D_MODEL), 0)

    def body(gi, h):
        g = (n_groups - 1 - gi) if reverse else gi
        off = pl.multiple_of(g * 8, 8)
        a = a_ref[pl.ds(off, 8), :]
        b = b_ref[pl.ds(off, 8), :]
        for k in (1, 2, 4):
            shift = (8 - k) if reverse else k
            msk = (row < 8 - k) if reverse else (row >= k)
            a_sh = pltpu.roll(a, shift, 0)
            b_sh = pltpu.roll(b, shift, 0)
            b = jnp.where(msk, a * b_sh + b, b)
            a = jnp.where(msk, a * a_sh, a)
        hg = a * h + b
        r_ref[pl.ds(off, 8), :] = hg
        return hg[0:1] if reverse else hg[7:8]

    return lax.fori_loop(0, n_groups, body, h_in)


def _rnn_in_kernel(x_ref, xp_ref, xn_ref, sc_ref, sh_ref, w_ref, cw_ref, cb_ref,
                   wgate_ref, bgate_ref, ap_ref, h0_ref,
                   r_ref, y_ref, xb_ref, h_ref, a_ref, b_ref, carry_ref, *, tt, reverse):
    i = pl.program_id(1)

    @pl.when(i == 0)
    def _():
        carry_ref[...] = h0_ref[0]

    _fill_modulated_dir(h_ref, x_ref, xp_ref, xn_ref, sc_ref, sh_ref, tt, reverse)
    h = h_ref[...]
    y_ref[0] = jnp.dot(h[HALO:HALO + tt], w_ref[:, :D_MODEL], preferred_element_type=F32).astype(BF16)
    u = jnp.dot(h, w_ref[:, D_MODEL:], preferred_element_type=F32)
    u3 = u.reshape((tt + 2 * HALO) // 8, 8, D_MODEL)
    sub = lax.broadcasted_iota(jnp.int32, u3.shape, 1)

    def shifted(k):
        rolled = pltpu.roll(u3, k % 8, 1)
        if k > 0:
            return jnp.where(sub < k, jnp.concatenate([rolled[-1:], rolled[:-1]], axis=0), rolled)
        return jnp.where(sub >= 8 + k, jnp.concatenate([rolled[1:], rolled[:1]], axis=0), rolled)

    full = (cw_ref[0][None] * shifted(2) + cw_ref[1][None] * shifted(1) + cw_ref[2][None] * u3
            + cw_ref[3][None] * shifted(-1) + cb_ref[...][None])
    xb = full.reshape(tt + 2 * HALO, D_MODEL)[HALO:HALO + tt]
    xb_ref[0] = xb
    _rg_coeffs(xb, wgate_ref, bgate_ref, ap_ref, a_ref, b_ref)
    carry_ref[...] = _scan_tile(a_ref, b_ref, r_ref.at[0], carry_ref[...], tt, reverse)


def _fill_modulated_dir(h_ref, x_ref, xp_ref, xn_ref, sc_ref, sh_ref, tt, reverse):
    i = pl.program_id(1)
    last = pl.num_programs(1) - 1
    ti = (last - i) if reverse else i
    scale = 1.0 + sc_ref[0]
    shift = sh_ref[0]
    h_ref[HALO:HALO + tt] = (x_ref[0] * scale + shift).astype(BF16)
    prev = xp_ref[0] * scale + shift
    nxt = xn_ref[0] * scale + shift
    h_ref[0:HALO] = jnp.where(ti > 0, prev, 0.0).astype(BF16)
    h_ref[HALO + tt:] = jnp.where(ti < last, nxt, 0.0).astype(BF16)


def _rnn_in(x, sc, sh, w_in, conv_w, conv_b, wgate, bgate, ap, h0, tt, reverse):
    b, t, _ = x.shape
    nt = t // tt
    nb = t // HALO
    per = tt // HALO

    def tix(i):
        return (nt - 1 - i) if reverse else i

    main = pl.BlockSpec((1, tt, D_MODEL), lambda bi, i: (bi, tix(i), 0))
    prev = pl.BlockSpec((1, HALO, D_MODEL), lambda bi, i: (bi, jnp.maximum(tix(i) * per - 1, 0), 0))
    nxt = pl.BlockSpec((1, HALO, D_MODEL), lambda bi, i: (bi, jnp.minimum((tix(i) + 1) * per, nb - 1), 0))
    vec = pl.BlockSpec((1, 1, D_MODEL), lambda bi, i: (bi, 0, 0))
    return pl.pallas_call(
        functools.partial(_rnn_in_kernel, tt=tt, reverse=reverse),
        out_shape=(jax.ShapeDtypeStruct((b, t, D_MODEL), F32),
                   jax.ShapeDtypeStruct((b, t, D_MODEL), BF16),
                   jax.ShapeDtypeStruct((b, t, D_MODEL), F32)),
        grid=(b, nt),
        in_specs=[main, prev, nxt, vec, vec,
                  _const_spec(w_in.shape), _const_spec(conv_w.shape), _const_spec(conv_b.shape),
                  _const_spec(wgate.shape), _const_spec(bgate.shape), _const_spec((1, D_MODEL)), vec],
        out_specs=(main, main, main),
        scratch_shapes=[pltpu.VMEM((tt + 2 * HALO, D_MODEL), BF16),
                        pltpu.VMEM((tt, D_MODEL), F32),
                        pltpu.VMEM((tt, D_MODEL), F32),
                        pltpu.VMEM((1, D_MODEL), F32)],
        compiler_params=_cparams(2), name="rnn_in_rev" if reverse else "rnn_in_fwd",
    )(x, x, x, sc, sh, w_in, conv_w, conv_b, wgate, bgate, ap, h0)


def _rnn_out_kernel(xb_ref, y_ref, rf_ref, x_ref, gate_ref, wgate_ref, bgate_ref, ap_ref, h0_ref,
                    w_ref, lg_ref, lb_ref, o_ref, a_ref, b_ref, rb_ref, carry_ref, *, tt):
    i = pl.program_id(1)

    @pl.when(i == 0)
    def _():
        carry_ref[...] = h0_ref[0]

    _rg_coeffs(xb_ref[0], wgate_ref, bgate_ref, ap_ref, a_ref, b_ref)
    carry_ref[...] = _scan_tile(a_ref, b_ref, rb_ref, carry_ref[...], tt, True)
    r = rf_ref[0] + rb_ref[...]
    m = (r * jax.nn.gelu(y_ref[0].astype(F32), approximate=True)).astype(BF16)
    o = jnp.dot(m, w_ref[...], preferred_element_type=F32)
    y = ALPHA * x_ref[0] + gate_ref[0] * o
    o_ref[0] = _layer_norm(y, lg_ref[...], lb_ref[...])


def _rnn_out(xb, y, rf, x, gate, wgate, bgate, ap, h0, w_out, ln_g, ln_b, tt):
    b, t, _ = x.shape
    nt = t // tt
    main = pl.BlockSpec((1, tt, D_MODEL), lambda bi, i: (bi, nt - 1 - i, 0))
    vec = pl.BlockSpec((1, 1, D_MODEL), lambda bi, i: (bi, 0, 0))
    par = _const_spec((1, D_MODEL))
    return pl.pallas_call(
        functools.partial(_rnn_out_kernel, tt=tt),
        out_shape=jax.ShapeDtypeStruct((b, t, D_MODEL), F32),
        grid=(b, nt),
        in_specs=[main, main, main, main, vec,
                  _const_spec(wgate.shape), _const_spec(bgate.shape), par, vec,
                  _const_spec(w_out.shape), par, par],
        out_specs=main,
        scratch_shapes=[pltpu.VMEM((tt, D_MODEL), F32)] * 3 + [pltpu.VMEM((1, D_MODEL), F32)],
        compiler_params=_cparams(2), name="rnn_out",
    )(xb, y, rf, x, gate, wgate, bgate, ap, h0, w_out, ln_g, ln_b)


def _rope_tables(n_tok):
    t = jnp.arange(n_tok)
    row = (t // GRID_W).astype(F32)[:, None]
    col = (t % GRID_W).astype(F32)[:, None]
    nf = HD // 4
    inv = 1.0 / (ROPE_THETA ** (jnp.arange(nf, dtype=F32) / nf))
    ang = jnp.concatenate([row * inv, row * inv, col * inv, col * inv], -1)
    sign = jnp.where((jnp.arange(HD) % (HD // 2)) < nf, -1.0, 1.0)
    reps = D_A // HD
    return jnp.tile(jnp.cos(ang), (1, reps)), jnp.tile(jnp.sin(ang) * sign, (1, reps))


def _tile(n, pref):
    return pref if n % pref == 0 else n


def kernel(x, c, ctx, c_ctx, ada_w, ada_b, ln1_g, ln1_b, ln2_g, ln2_b, ffn_w_up, ffn_conv_w, ffn_conv_b, ffn_w_down, att_w_in, att_w_out, diff_lq1, diff_lk1, diff_lq2, diff_lk2, diff_subln_g, na_rpb, rnn_w_in, rnn_conv_w, rnn_conv_b, rg_a_param, rg_wa, rg_ba, rg_wx, rg_bx, rnn_w_out):
    bsz, t, _ = x.shape
    n_ctx = ctx.shape[1]
    assert t % GRID_W == 0 and t // GRID_W >= WIN_R and n_ctx % HALO == 0

    n_rows = ((bsz + 1 + 7) // 8) * 8
    cc = jnp.zeros((n_rows, D_MODEL), F32).at[:bsz].set(c).at[bsz].set(c_ctx)
    mod = _ada_mod(cc, ada_w, ada_b)

    def mods(l):
        lat = mod[l, :bsz].reshape(bsz, 1, 6, D_MODEL)
        cx = jnp.broadcast_to(mod[l, bsz].reshape(1, 1, 6, D_MODEL), (bsz, 1, 6, D_MODEL))
        return ([lat[:, :, k] for k in range(6)], [cx[:, :, k] for k in range(6)])

    tm = _tile(t, 512)
    tmc = _tile(n_ctx, 512)

    (sh1, sc1, g1, sh2, sc2, g2), (csh1, csc1, cg1, csh2, csc2, cg2) = mods(0)
    lambda_init = 0.8 - 0.6 * math.exp(-0.3 * 0)
    w_in = att_w_in[0].astype(BF16)
    w_out = att_w_out[0].astype(BF16)
    cos_t, sin_t = _rope_tables(t)
    qa, ka, va, qb, kb, vb = _proj_attn(x, sc1, sh1, w_in, cos_t, sin_t, True, tm)
    qa_c, ka_c, va_c, qb_c, kb_c, vb_c = _proj_attn(
        ctx, csc1, csh1, w_in, cos_t[:n_ctx], sin_t[:n_ctx], False, tmc)
    lp = jnp.stack([diff_lq1[0], diff_lk1[0], diff_lq2[0], diff_lk2[0]])
    sub_g = diff_subln_g[0].reshape(1, 2 * HD)
    oa = _diff_attn(lp, sub_g, qa, ka_c, va_c, ka, va, lambda_init, _tile(t, 1024), _tile(t, 512))
    ob = _na_attn(qb, kb, vb, kb_c, vb_c, _na_bias_table(na_rpb[0]))
    oa_c, ob_c = _ctx_attn(lp, sub_g, qa_c, ka_c, va_c, qb_c, kb_c, vb_c, lambda_init)
    l1g, l1b = ln1_g[0].reshape(1, D_MODEL), ln1_b[0].reshape(1, D_MODEL)
    l2g, l2b = ln2_g[0].reshape(1, D_MODEL), ln2_b[0].reshape(1, D_MODEL)
    x = _outproj_ln(oa, ob, w_out, x, g1, l1g, l1b, tm)
    ctx = _outproj_ln(oa_c, ob_c, w_out, ctx, cg1, l1g, l1b, tmc)
    fw = _prep_ffn(ffn_w_up[0], ffn_conv_w[0], ffn_conv_b[0], ffn_w_down[0])
    x = _conv_ffn_ln(x, sc2, sh2, g2, fw, l2g, l2b, tm)
    ctx = _conv_ffn_ln(ctx, csc2, csh2, cg2, fw, l2g, l2b, tmc)

    (sh1, sc1, g1, sh2, sc2, g2), (csh1, csc1, _, _, _, _) = mods(1)
    w_in = rnn_w_in[0].astype(BF16)
    w_out = rnn_w_out[0].astype(BF16)
    conv_w = jnp.broadcast_to(rnn_conv_w[0][:, None, :], (CONV_RNN, 8, D_MODEL))
    conv_b = jnp.broadcast_to(rnn_conv_b[0][None, :], (8, D_MODEL))
    wgate = [(0.5 * jnp.concatenate([rg_wa[0, d], rg_wx[0, d]], axis=-1)).astype(BF16) for d in range(2)]
    bgate = [0.5 * jnp.concatenate([rg_ba[0, d], rg_bx[0, d]], axis=-1).reshape(N_RG_BLOCKS, 1, 2 * RG_BW)
             for d in range(2)]
    ap = [rg_a_param[0, d].reshape(1, D_MODEL) for d in range(2)]
    zeros = jnp.zeros((bsz, 1, D_MODEL), F32)
    tt = _tile(t, 512)
    ttc = _tile(n_ctx, 512)
    rc_f, _, _ = _rnn_in(ctx, csc1, csh1, w_in, conv_w, conv_b, wgate[0], bgate[0], ap[0],
                         zeros, ttc, False)
    rc_b, _, _ = _rnn_in(ctx, csc1, csh1, w_in, conv_w, conv_b, wgate[1], bgate[1], ap[1],
                         zeros, ttc, True)
    h0_f = rc_f[:, n_ctx - 1:n_ctx]
    h0_b = rc_b[:, 0:1]
    r_f, y, xb = _rnn_in(x, sc1, sh1, w_in, conv_w, conv_b, wgate[0], bgate[0], ap[0],
                         h0_f, tt, False)
    l1g, l1b = ln1_g[1].reshape(1, D_MODEL), ln1_b[1].reshape(1, D_MODEL)
    l2g, l2b = ln2_g[1].reshape(1, D_MODEL), ln2_b[1].reshape(1, D_MODEL)
    x = _rnn_out(xb, y, r_f, x, g1, wgate[1], bgate[1], ap[1], h0_b, w_out, l1g, l1b, tt)
    fw = _prep_ffn(ffn_w_up[1], ffn_conv_w[1], ffn_conv_b[1], ffn_w_down[1])
    x = _conv_ffn_ln(x, sc2, sh2, g2, fw, l2g, l2b, tm)
    return x
```

```python
import functools
import math

import jax
import jax.numpy as jnp
from jax import lax
from jax.experimental import pallas as pl
from jax.experimental.pallas import tpu as pltpu

F32 = jnp.float32
BF16 = jnp.bfloat16

D_MODEL = 1024
DEPTH = 2
GRID_W = 64
HD = 64
D_A = D_MODEL // 2
D_B = D_MODEL - D_A
H_A = D_A // (2 * HD)
H_B = D_B // HD
WIN_R = 8
WIN_C = 16
ROPE_THETA = 10000.0
N_RG_BLOCKS = 8
RG_BW = D_MODEL // N_RG_BLOCKS
RG_C = 8.0
CONV_RNN = 4
D_FF = ((8 * D_MODEL // 3 + 127) // 128) * 128
CONV_FFN = 3
ALPHA = (2 * DEPTH) ** 0.25
NORM_EPS = 1e-5
NEG_BIG = -1e30
Q_SCALE = HD ** -0.5 * math.log2(math.e)
GELU_C1 = math.sqrt(2.0 / math.pi)
GELU_C2 = GELU_C1 * 0.044715

LANES = 128
HALO = 16
DENOM_ROWS = 16
FF_CHUNK = 256
N_FF_CHUNKS = D_FF // FF_CHUNK
VMEM_LIMIT = 56 * 1024 * 1024

_NT = (((1,), (1,)), ((), ()))


def _cparams(n_axes, flags=None):
    return pltpu.CompilerParams(dimension_semantics=("arbitrary",) * n_axes,
                                vmem_limit_bytes=VMEM_LIMIT, flags=flags)


def _layer_norm(y, g, b):
    mu = jnp.mean(y, axis=-1, keepdims=True)
    yc = y - mu
    var = jnp.mean(yc * yc, axis=-1, keepdims=True)
    return yc * lax.rsqrt(var + NORM_EPS) * g + b


def _softmax_parts(s_list):
    m = s_list[0].max(axis=-1, keepdims=True)
    for s in s_list[1:]:
        m = jnp.maximum(m, s.max(axis=-1, keepdims=True))
    ps = [jnp.exp2(s - m) for s in s_list]
    l = ps[0].sum(axis=-1, keepdims=True)
    for p in ps[1:]:
        l = l + p.sum(axis=-1, keepdims=True)
    return ps, 1.0 / l


def _half_masks(dtype):
    lane = lax.broadcasted_iota(jnp.int32, (1, LANES), 1)
    lo = (lane < HD).astype(dtype)
    return lo, (1 - lo).astype(dtype)


def _ada_kernel(c_ref, w_ref, b_ref, o_ref):
    c = c_ref[...]
    a = c * jax.nn.sigmoid(c)
    o_ref[0] = jnp.dot(a, w_ref[0], preferred_element_type=F32,
                       precision=lax.Precision.HIGHEST) + b_ref[0]


def _ada_mod(cc, ada_w, ada_b):
    n = cc.shape[0]
    tn = 1536
    return pl.pallas_call(
        _ada_kernel,
        out_shape=jax.ShapeDtypeStruct((DEPTH, n, 6 * D_MODEL), F32),
        grid=(DEPTH, 6 * D_MODEL // tn),
        in_specs=[pl.BlockSpec((n, D_MODEL), lambda l, j: (0, 0)),
                  pl.BlockSpec((1, D_MODEL, tn), lambda l, j: (l, 0, j)),
                  pl.BlockSpec((1, 1, tn), lambda l, j: (l, 0, j))],
        out_specs=pl.BlockSpec((1, n, tn), lambda l, j: (l, 0, j)),
        compiler_params=_cparams(2), name="ada_mod",
    )(cc, ada_w, ada_b.reshape(DEPTH, 1, 6 * D_MODEL))


def _proj_attn_kernel(x_ref, sc_ref, sh_ref, w_ref, cos_ref, sin_ref,
                      qa_ref, ka_ref, va_ref, qb_ref, kb_ref, vb_ref, vt_ref, *, rope):
    h = (x_ref[0] * (1.0 + sc_ref[0]) + sh_ref[0]).astype(BF16)
    outs = (qa_ref, ka_ref, va_ref, qb_ref, kb_ref, vb_ref)
    for j, o_ref in enumerate(outs):
        r = jnp.dot(h, w_ref[:, j * D_A:(j + 1) * D_A], preferred_element_type=F32)
        if rope and j < 2:
            lane = lax.broadcasted_iota(jnp.int32, r.shape, 1)
            first_half = (lane % (HD // 2)) < (HD // 4)
            rot = jnp.where(first_half, pltpu.roll(r, D_A - HD // 4, 1), pltpu.roll(r, HD // 4, 1))
            r = r * cos_ref[...] + rot * sin_ref[...]
        if j in (0, 3):
            r = r * Q_SCALE
        if j in (2, 5):
            vt_ref[...] = r
            o_ref[0] = vt_ref[...].T.astype(BF16)
        else:
            o_ref[0] = r.astype(BF16)


def _proj_attn(x, sc, sh, w, cos_t, sin_t, rope, tm):
    b, t, _ = x.shape
    out = jax.ShapeDtypeStruct((b, t, D_A), BF16)
    out_t = jax.ShapeDtypeStruct((b, D_A, t), BF16)
    vec = pl.BlockSpec((1, 1, D_MODEL), lambda bi, i: (bi, 0, 0))
    blk = pl.BlockSpec((1, tm, D_A), lambda bi, i: (bi, i, 0))
    blk_t = pl.BlockSpec((1, D_A, tm), lambda bi, i: (bi, 0, i))
    tab = pl.BlockSpec((tm, D_A), lambda bi, i: (i, 0))
    return pl.pallas_call(
        functools.partial(_proj_attn_kernel, rope=rope),
        out_shape=(out, out, out_t, out, out, out_t),
        grid=(b, t // tm),
        in_specs=[pl.BlockSpec((1, tm, D_MODEL), lambda bi, i: (bi, i, 0)), vec, vec,
                  pl.BlockSpec((D_MODEL, 6 * D_A), lambda bi, i: (0, 0)), tab, tab],
        out_specs=(blk, blk, blk_t, blk, blk, blk_t),
        scratch_shapes=[pltpu.VMEM((tm, D_A), F32)],
        compiler_params=_cparams(2), name="proj_attn",
    )(x, sc, sh, w, cos_t, sin_t)


def _lambda_full(lp_ref, lambda_init):
    lp = lp_ref[...]
    s1 = jnp.sum(lp[0:1] * lp[1:2], axis=-1, keepdims=True)
    s2 = jnp.sum(lp[2:3] * lp[3:4], axis=-1, keepdims=True)
    return jnp.exp(s1) - jnp.exp(s2) + lambda_init


def _head_rms(o, g, lambda_init):
    o = o * lax.rsqrt(jnp.mean(o * o, axis=-1, keepdims=True) + NORM_EPS)
    return o * g * (1.0 - lambda_init)


def _diff_attn_kernel(lp_ref, g_ref, q_ref, kc_ref, vct_ref, kl_ref, vlt_ref, o_ref,
                      s_ref, p_ref, acc_ref, *, tk, n_chunks, n_ctx, lambda_init):
    lam = _lambda_full(lp_ref, lambda_init)
    q = q_ref[0]
    lo, hi = _half_masks(BF16)
    qs = (q * lo, q * hi)
    tq = q.shape[0]
    chunks = [(kc_ref, vct_ref, 0, n_ctx)] + [(kl_ref, vlt_ref, j * tk, tk) for j in range(n_chunks)]
    n_all = len(chunks)
    acc_ref[...] = jnp.zeros_like(acc_ref)

    def scores(c):
        k_ref, _, off, size = chunks[c]
        k = k_ref[0, off:off + size, :]
        for mi in range(2):
            s_ref[c % 2, mi, :size, :] = lax.dot_general(k, qs[mi], _NT, preferred_element_type=F32)

    def softmax(c, m_old):
        size = chunks[c][3]
        m_out, a_out = [], []
        for mi in range(2):
            s = s_ref[c % 2, mi, :size, :]
            m_new = jnp.maximum(m_old[mi], s.max(axis=0, keepdims=True))
            a_out.append(jnp.exp2(m_old[mi] - m_new))
            p_ref[c % 2, mi, :size, :] = jnp.exp2(s - m_new).astype(BF16)
            m_out.append(m_new)
        return m_out, a_out

    def values(c, a):
        _, vt_ref, off, size = chunks[c]
        vext = jnp.concatenate([vt_ref[0, :, off:off + size], jnp.ones((DENOM_ROWS, size), BF16)], axis=0)
        for mi in range(2):
            pv = jnp.dot(vext, p_ref[c % 2, mi, :size, :], preferred_element_type=F32)
            acc_ref[mi] = a[mi] * acc_ref[mi] + pv

    m = [jnp.full((1, tq), NEG_BIG, F32)] * 2
    a_next = None
    for i in range(-2, n_all):
        a_cur = a_next
        if i + 2 < n_all:
            scores(i + 2)
        if 0 <= i + 1 < n_all:
            m, a_next = softmax(i + 1, m)
        if i >= 0:
            values(i, a_cur)

    outs = []
    for mi in range(2):
        acc = acc_ref[mi]
        outs.append(acc[:2 * HD] / acc[2 * HD:2 * HD + 1])
    o = (outs[0] - lam * outs[1]).T
    o_ref[0] = _head_rms(o, g_ref[...], lambda_init).astype(BF16)


def _diff_attn(lp, g, qa, ka_c, vat_c, ka, vat, lambda_init, tq, tk):
    b, t, _ = qa.shape
    n_ctx = ka_c.shape[1]
    assert n_ctx <= tk and n_ctx % LANES == 0
    q_blk = pl.BlockSpec((1, tq, LANES), lambda bi, h, i: (bi, i, h))
    return pl.pallas_call(
        functools.partial(_diff_attn_kernel, tk=tk, n_chunks=t // tk, n_ctx=n_ctx, lambda_init=lambda_init),
        out_shape=jax.ShapeDtypeStruct((b, t, D_A), BF16),
        grid=(b, H_A, t // tq),
        in_specs=[pl.BlockSpec((4, HD), lambda bi, h, i: (0, 0)),
                  pl.BlockSpec((1, 2 * HD), lambda bi, h, i: (0, 0)),
                  q_blk,
                  pl.BlockSpec((1, n_ctx, LANES), lambda bi, h, i: (bi, 0, h)),
                  pl.BlockSpec((1, LANES, n_ctx), lambda bi, h, i: (bi, h, 0)),
                  pl.BlockSpec((1, t, LANES), lambda bi, h, i: (bi, 0, h)),
                  pl.BlockSpec((1, LANES, t), lambda bi, h, i: (bi, h, 0))],
        out_specs=q_blk,
        scratch_shapes=[pltpu.VMEM((2, 2, tk, tq), F32),
                        pltpu.VMEM((2, 2, tk, tq), BF16),
                        pltpu.VMEM((2, 2 * HD + DENOM_ROWS, tq), F32)],
        compiler_params=_cparams(3), name="diff_attn",
    )(lp, g, qa, ka_c, vat_c, ka, vat)


NA_QROWS = 4
NA_KROWS = NA_QROWS + WIN_R
NA_REL_PAD = NA_QROWS
NA_REL_N = NA_KROWS + WIN_R + NA_QROWS - 2


def _na_kernel(q_ref, k_ref, vt_ref, kc_ref, vct_ref, tab_ref, o_ref, s_ref, p_ref, ot_ref, *, rows, n_ctx):
    g = pl.program_id(1)
    r0 = g * NA_QROWS
    ws = jnp.clip(r0 - WIN_R // 2, 0, rows - NA_KROWS)
    d0 = r0 - ws
    k_start = pl.multiple_of(ws * GRID_W, NA_QROWS * GRID_W)
    n_loc = NA_KROWS * GRID_W
    nq = NA_QROWS * GRID_W
    lo, hi = _half_masks(BF16)

    qr = lax.broadcasted_iota(jnp.int32, (1, nq), 1) // GRID_W
    rs = jnp.clip(r0 + qr - WIN_R // 2, 0, rows - WIN_R)
    row_mask = []
    for kr in range(NA_KROWS):
        valid = (ws + kr >= rs) & (ws + kr < rs + WIN_R)
        row_mask.append(jnp.where(valid, 0.0, NEG_BIG))

    def scores(h):
        hp, e = divmod(h, 2)
        cols = slice(hp * LANES, (hp + 1) * LANES)
        qm = q_ref[0, :, cols] * (lo, hi)[e]
        s_loc = lax.dot_general(k_ref[0, pl.ds(k_start, n_loc), cols], qm, _NT, preferred_element_type=F32)
        for kr in range(NA_KROWS):
            base = kr - d0 + (WIN_R - 1) + (NA_REL_PAD - 1)
            bias = jnp.concatenate([tab_ref[h, base - 2 * p] for p in range(NA_QROWS // 2)], axis=1)
            rows_kr = slice(kr * GRID_W, (kr + 1) * GRID_W)
            s_ref[h % 2, rows_kr, :] = s_loc[rows_kr] + bias + row_mask[kr]
        s_ref[h % 2, n_loc:, :] = lax.dot_general(kc_ref[0, :, cols], qm, _NT, preferred_element_type=F32)

    def softmax(h):
        s = s_ref[h % 2]
        m = s.max(axis=0, keepdims=True)
        p_ref[h % 2] = jnp.exp2(s - m).astype(BF16)

    def values(h):
        hrows = slice(h * HD, (h + 1) * HD)
        vt = jnp.concatenate([vt_ref[0, hrows, pl.ds(k_start, n_loc)], vct_ref[0, hrows, :]], axis=1)
        vext = jnp.concatenate([vt, jnp.ones((DENOM_ROWS, n_loc + n_ctx), BF16)], axis=0)
        o = jnp.dot(vext, p_ref[h % 2], preferred_element_type=F32)
        ot_ref[hrows, :] = o[:HD] / o[HD:HD + 1]

    for i in range(-2, H_B):
        if i + 2 < H_B:
            scores(i + 2)
        if 0 <= i + 1 < H_B:
            softmax(i + 1)
        if i >= 0:
            values(i)
    o_ref[0] = ot_ref[...].T.astype(BF16)


def _na_bias_table(rpb):
    c = jnp.arange(GRID_W)[None, :]
    kc = jnp.arange(GRID_W)[:, None]
    cstart = jnp.clip(c - WIN_C // 2, 0, GRID_W - WIN_C)
    inside = (kc >= cstart) & (kc < cstart + WIN_C)
    rel_c = jnp.clip(kc - c + (WIN_C - 1), 0, 2 * WIN_C - 2)
    band = jnp.where(inside, rpb[:, :, rel_c] * math.log2(math.e), NEG_BIG)
    hi_pad = NA_REL_N - NA_REL_PAD + 1 - (2 * WIN_R - 1)
    band = jnp.pad(band, ((0, 0), (NA_REL_PAD, hi_pad), (0, 0), (0, 0)), constant_values=NEG_BIG)
    return jnp.concatenate([band[:, 1:], band[:, :-1]], axis=-1).astype(F32)


def _na_attn(qb, kb, vbt, kb_c, vbt_c, tab):
    b, t, _ = qb.shape
    rows = t // GRID_W
    n_ctx = kb_c.shape[1]
    assert rows % NA_QROWS == 0 and rows >= NA_KROWS and tab.shape[1] == NA_REL_N
    nq = NA_QROWS * GRID_W
    n_keys = NA_KROWS * GRID_W + n_ctx
    qblk = pl.BlockSpec((1, nq, D_B), lambda bi, g: (bi, g, 0))
    return pl.pallas_call(
        functools.partial(_na_kernel, rows=rows, n_ctx=n_ctx),
        out_shape=jax.ShapeDtypeStruct((b, t, D_B), BF16),
        grid=(b, rows // NA_QROWS),
        in_specs=[qblk,
                  pl.BlockSpec((1, t, D_B), lambda bi, g: (bi, 0, 0)),
                  pl.BlockSpec((1, D_B, t), lambda bi, g: (bi, 0, 0)),
                  pl.BlockSpec((1, n_ctx, D_B), lambda bi, g: (bi, 0, 0)),
                  pl.BlockSpec((1, D_B, n_ctx), lambda bi, g: (bi, 0, 0)),
                  pl.BlockSpec(tab.shape, lambda bi, g: (0, 0, 0, 0))],
        out_specs=qblk,
        scratch_shapes=[pltpu.VMEM((2, n_keys, nq), F32),
                        pltpu.VMEM((2, n_keys, nq), BF16),
                        pltpu.VMEM((D_B, nq), F32)],
        compiler_params=_cparams(2), name="na_attn",
    )(qb, kb, vbt, kb_c, vbt_c, tab)


def _ctx_attn_kernel(lp_ref, g_ref, qa_ref, ka_ref, vat_ref, qb_ref, kb_ref, vbt_ref,
                     oa_ref, ob_ref, *, lambda_init):
    lam = _lambda_full(lp_ref, lambda_init)
    lo, hi = _half_masks(BF16)
    lane = lax.broadcasted_iota(jnp.int32, (1, LANES), 1)
    for h in range(H_A):
        cols = slice(h * LANES, (h + 1) * LANES)
        q, k, vt = qa_ref[0, :, cols], ka_ref[0, :, cols], vat_ref[0, cols, :]
        probs = []
        for msk in (lo, hi):
            s = lax.dot_general(q * msk, k, _NT, preferred_element_type=F32)
            (p,), inv = _softmax_parts([s])
            probs.append(p * inv)
        w = (probs[0] - lam * probs[1]).astype(BF16)
        o = lax.dot_general(w, vt, _NT, preferred_element_type=F32)
        oa_ref[0, :, cols] = _head_rms(o, g_ref[...], lambda_init).astype(BF16)
    for hp in range(H_B // 2):
        cols = slice(hp * LANES, (hp + 1) * LANES)
        q, k, vt = qb_ref[0, :, cols], kb_ref[0, :, cols], vbt_ref[0, cols, :]
        halves = []
        for msk in (lo, hi):
            s = lax.dot_general(q * msk, k, _NT, preferred_element_type=F32)
            (p,), inv = _softmax_parts([s])
            halves.append(lax.dot_general((p * inv).astype(BF16), vt, _NT, preferred_element_type=F32))
        ob_ref[0, :, cols] = jnp.where(lane < HD, halves[0], halves[1]).astype(BF16)


def _ctx_attn(lp, g, qa, ka, vat, qb, kb, vbt, lambda_init):
    b, n, _ = qa.shape
    blk = pl.BlockSpec((1, n, D_A), lambda bi: (bi, 0, 0))
    blk_t = pl.BlockSpec((1, D_A, n), lambda bi: (bi, 0, 0))
    out = jax.ShapeDtypeStruct((b, n, D_A), BF16)
    return pl.pallas_call(
        functools.partial(_ctx_attn_kernel, lambda_init=lambda_init),
        out_shape=(out, out),
        grid=(b,),
        in_specs=[pl.BlockSpec((4, HD), lambda bi: (0, 0)),
                  pl.BlockSpec((1, 2 * HD), lambda bi: (0, 0)),
                  blk, blk, blk_t, blk, blk, blk_t],
        out_specs=(blk, blk),
        compiler_params=_cparams(1), name="ctx_attn",
    )(lp, g, qa, ka, vat, qb, kb, vbt)


def _outproj_ln_kernel(oa_ref, ob_ref, w_ref, x_ref, gate_ref, g_ref, b_ref, o_ref):
    o = (jnp.dot(oa_ref[0], w_ref[:D_A], preferred_element_type=F32)
         + jnp.dot(ob_ref[0], w_ref[D_A:], preferred_element_type=F32))
    y = ALPHA * x_ref[0] + gate_ref[0] * o
    o_ref[0] = _layer_norm(y, g_ref[...], b_ref[...])


def _outproj_ln(oa, ob, w, x, gate, ln_g, ln_b, tm):
    b, t, _ = x.shape
    half = pl.BlockSpec((1, tm, D_A), lambda bi, i: (bi, i, 0))
    full = pl.BlockSpec((1, tm, D_MODEL), lambda bi, i: (bi, i, 0))
    vec = pl.BlockSpec((1, 1, D_MODEL), lambda bi, i: (bi, 0, 0))
    par = pl.BlockSpec((1, D_MODEL), lambda bi, i: (0, 0))
    return pl.pallas_call(
        _outproj_ln_kernel,
        out_shape=jax.ShapeDtypeStruct((b, t, D_MODEL), F32),
        grid=(b, t // tm),
        in_specs=[half, half, pl.BlockSpec((D_MODEL, D_MODEL), lambda bi, i: (0, 0)),
                  full, vec, par, par],
        out_specs=full,
        compiler_params=_cparams(2), name="outproj_ln",
    )(oa, ob, w, x, gate, ln_g, ln_b)


def _fill_modulated(h_ref, x_ref, xp_ref, xn_ref, sc_ref, sh_ref, tm):
    i = pl.program_id(1)
    last = pl.num_programs(1) - 1
    scale = 1.0 + sc_ref[0]
    shift = sh_ref[0]
    h_ref[HALO:HALO + tm] = (x_ref[0] * scale + shift).astype(BF16)
    prev = xp_ref[0] * scale + shift
    nxt = xn_ref[0] * scale + shift
    h_ref[0:HALO] = jnp.where(i > 0, prev, 0.0).astype(BF16)
    h_ref[HALO + tm:] = jnp.where(i < last, nxt, 0.0).astype(BF16)


def _conv_ffn_kernel(x_ref, xp_ref, xn_ref, sc_ref, sh_ref, gate_ref,
                     wg_ref, wv_ref, cwg_ref, cbg_ref, cwv_ref, cbv_ref, wd_ref,
                     lg_ref, lb_ref, o_ref, h_ref, ug_ref, uv_ref, a_ref, *, tm):
    _fill_modulated(h_ref, x_ref, xp_ref, xn_ref, sc_ref, sh_ref, tm)

    def conv3(u_ref, slot, cw, cb):
        u = u_ref[slot]
        rows, cols = u.shape
        u3 = u.reshape(rows // 8, 8, cols)
        sub = lax.broadcasted_iota(jnp.int32, u3.shape, 1)
        dn = pltpu.roll(u3, 1, 1)
        dn = jnp.where(sub == 0, jnp.concatenate([dn[-1:], dn[:-1]], axis=0), dn)
        up_ = pltpu.roll(u3, 7, 1)
        up_ = jnp.where(sub == 7, jnp.concatenate([up_[1:], up_[:1]], axis=0), up_)
        full = cw[0][None] * dn + cw[1][None] * u3 + cw[2][None] * up_ + cb[0][None]
        return full.reshape(rows, cols)[HALO:HALO + tm]

    def up(c, slot):
        h = h_ref[...]
        ug_ref[slot] = jnp.dot(h, wg_ref[c], preferred_element_type=F32)
        uv_ref[slot] = jnp.dot(h, wv_ref[c], preferred_element_type=F32)

    def act(c, slot):
        g = conv3(ug_ref, slot, cwg_ref[c], cbg_ref[c])
        v = conv3(uv_ref, slot, cwv_ref[c], cbv_ref[c])
        col = c * FF_CHUNK
        if not isinstance(c, int):
            col = pl.multiple_of(col, FF_CHUNK)
        t = jnp.tanh(g * (GELU_C1 + GELU_C2 * (g * g)))
        a_ref[:, pl.ds(col, FF_CHUNK)] = ((g + g * t) * v).astype(BF16)

    n = N_FF_CHUNKS
    up(0, 0)
    n_pairs = 0

    def body(j, _):
        t = 1 + 2 * j
        up(t, 1)
        act(t - 1, 0)
        up(t + 1, 0)
        act(t, 1)
        return 0

    lax.fori_loop(0, n_pairs, body, 0)
    for t in range(1 + 2 * n_pairs, n):
        up(t, t % 2)
        act(t - 1, (t - 1) % 2)
    act(n - 1, (n - 1) % 2)
    o = jnp.dot(a_ref[...], wd_ref[...], preferred_element_type=F32)
    y = ALPHA * x_ref[0] + gate_ref[0] * o
    o_ref[0] = _layer_norm(y, lg_ref[...], lb_ref[...])


def _halo_specs(t, tm):
    nb = t // HALO
    per = tm // HALO
    main = pl.BlockSpec((1, tm, D_MODEL), lambda bi, i: (bi, i, 0))
    prev = pl.BlockSpec((1, HALO, D_MODEL), lambda bi, i: (bi, jnp.maximum(i * per - 1, 0), 0))
    nxt = pl.BlockSpec((1, HALO, D_MODEL), lambda bi, i: (bi, jnp.minimum((i + 1) * per, nb - 1), 0))
    return main, prev, nxt


def _const_spec(shape):
    nd = len(shape)
    return pl.BlockSpec(shape, lambda bi, i: (0,) * nd, pipeline_mode=pl.Buffered(1))


def _conv_ffn_ln(x, sc, sh, gate, fw, ln_g, ln_b, tm):
    b, t, _ = x.shape
    main, prev, nxt = _halo_specs(t, tm)
    vec = pl.BlockSpec((1, 1, D_MODEL), lambda bi, i: (bi, 0, 0))
    par = _const_spec((1, D_MODEL))
    wg, wv, cwg, cbg, cwv, cbv, wd = fw
    return pl.pallas_call(
        functools.partial(_conv_ffn_kernel, tm=tm),
        out_shape=jax.ShapeDtypeStruct((b, t, D_MODEL), F32),
        grid=(b, t // tm),
        in_specs=[main, prev, nxt, vec, vec, vec,
                  _const_spec(wg.shape), _const_spec(wv.shape),
                  _const_spec(cwg.shape), _const_spec(cbg.shape),
                  _const_spec(cwv.shape), _const_spec(cbv.shape),
                  _const_spec(wd.shape), par, par],
        out_specs=main,
        scratch_shapes=[pltpu.VMEM((tm + 2 * HALO, D_MODEL), BF16),
                        pltpu.VMEM((2, tm + 2 * HALO, FF_CHUNK), F32),
                        pltpu.VMEM((2, tm + 2 * HALO, FF_CHUNK), F32),
                        pltpu.VMEM((tm, D_FF), BF16)],
        compiler_params=_cparams(2), name="conv_ffn_ln",
    )(x, x, x, sc, sh, gate, wg, wv, cwg, cbg, cwv, cbv, wd, ln_g, ln_b)


def _prep_ffn(w_up, conv_w, conv_b, w_down):
    def chunks_cols(a):
        return jnp.transpose(a.reshape(a.shape[0], N_FF_CHUNKS, FF_CHUNK), (1, 0, 2))
    wg = chunks_cols(w_up[:, :D_FF]).astype(BF16)
    wv = chunks_cols(w_up[:, D_FF:]).astype(BF16)
    def sublane_rep(a):
        return jnp.broadcast_to(a[:, :, None, :], a.shape[:2] + (8, FF_CHUNK))
    cwg = sublane_rep(chunks_cols(conv_w[:, :D_FF]))
    cwv = sublane_rep(chunks_cols(0.5 * conv_w[:, D_FF:]))
    cbg = sublane_rep(chunks_cols(conv_b[None, :D_FF]))
    cbv = sublane_rep(chunks_cols(0.5 * conv_b[None, D_FF:]))
    wd = w_down.astype(BF16)
    return wg, wv, cwg, cbg, cwv, cbv, wd


def _rg_coeffs(xb, wgate_ref, bgate_ref, ap_ref, a_ref, b_ref, reverse):
    ap = ap_ref[...]
    neg = -ap
    sp = jnp.maximum(neg, 0.0) + jnp.log1p(jnp.exp(-jnp.abs(neg)))
    half_c = (-0.5 * RG_C) * sp
    for n in range(N_RG_BLOCKS):
        cols = slice(n * RG_BW, (n + 1) * RG_BW)
        xn = xb[:, cols]
        t = jnp.tanh(jnp.dot(xn.astype(BF16), wgate_ref[n], preferred_element_type=F32) + bgate_ref[n])
        log_a = half_c[:, cols] + half_c[:, cols] * t[:, :RG_BW]
        gate_x = 0.5 + 0.5 * t[:, RG_BW:]
        a = jnp.exp(log_a)
        one_minus_a2 = -jnp.tanh(log_a) * (1.0 + a * a)
        a_ref[:, cols], b_ref[:, cols] = _group_prefix(a, gate_x * xn * jnp.sqrt(one_minus_a2), reverse)


def _group_prefix(a, b, reverse):
    rows, cols = a.shape
    a = a.reshape(rows // 8, 8, cols)
    b = b.reshape(rows // 8, 8, cols)
    sub = lax.broadcasted_iota(jnp.int32, a.shape, 1)
    for k in (1, 2, 4):
        shift = (8 - k) if reverse else k
        msk = (sub < 8 - k) if reverse else (sub >= k)
        a_sh = pltpu.roll(a, shift, 1)
        b_sh = pltpu.roll(b, shift, 1)
        b = jnp.where(msk, a * b_sh + b, b)
        a = jnp.where(msk, a * a_sh, a)
    return a.reshape(rows, cols), b.reshape(rows, cols)


def _scan_tile(a_ref, b_ref, r_ref, h_in, tt, reverse):
    n_groups = tt // 8
    h = h_in
    for gi in range(n_groups):
        g = (n_groups - 1 - gi) if reverse else gi
        rows = slice(g * 8, (g + 1) * 8)
        hg = a_ref[rows, :] * h + b_ref[rows, :]
        r_ref[rows, :] = hg
        h = hg[0:1] if reverse else hg[7:8]
    return h


def _rnn_in_kernel(x_ref, xp_ref, xn_ref, sc_ref, sh_ref, w_ref, cw_ref, cb_ref,
                   wgate_ref, bgate_ref, ap_ref, h0_ref,
                   r_ref, y_ref, xb_ref, h_ref, a_ref, b_ref, carry_ref, *, tt, reverse):
    i = pl.program_id(1)

    @pl.when(i == 0)
    def _():
        carry_ref[...] = h0_ref[0]

    _fill_modulated_dir(h_ref, x_ref, xp_ref, xn_ref, sc_ref, sh_ref, tt, reverse)
    h = h_ref[...]
    y_ref[0] = jnp.dot(h[HALO:HALO + tt], w_ref[:, :D_MODEL], preferred_element_type=F32).astype(BF16)
    u = jnp.dot(h, w_ref[:, D_MODEL:], preferred_element_type=F32)
    u3 = u.reshape((tt + 2 * HALO) // 8, 8, D_MODEL)
    sub = lax.broadcasted_iota(jnp.int32, u3.shape, 1)

    def shifted(k):
        rolled = pltpu.roll(u3, k % 8, 1)
        if k > 0:
            return jnp.where(sub < k, jnp.concatenate([rolled[-1:], rolled[:-1]], axis=0), rolled)
        return jnp.where(sub >= 8 + k, jnp.concatenate([rolled[1:], rolled[:1]], axis=0), rolled)

    full = (cw_ref[0][None] * shifted(2) + cw_ref[1][None] * shifted(1) + cw_ref[2][None] * u3
            + cw_ref[3][None] * shifted(-1) + cb_ref[...][None])
    xb = full.reshape(tt + 2 * HALO, D_MODEL)[HALO:HALO + tt]
    xb_ref[0] = xb
    _rg_coeffs(xb, wgate_ref, bgate_ref, ap_ref, a_ref, b_ref, reverse)
    carry_ref[...] = _scan_tile(a_ref, b_ref, r_ref.at[0], carry_ref[...], tt, reverse)


def _fill_modulated_dir(h_ref, x_ref, xp_ref, xn_ref, sc_ref, sh_ref, tt, reverse):
    i = pl.program_id(1)
    last = pl.num_programs(1) - 1
    ti = (last - i) if reverse else i
    scale = 1.0 + sc_ref[0]
    shift = sh_ref[0]
    h_ref[HALO:HALO + tt] = (x_ref[0] * scale + shift).astype(BF16)
    prev = xp_ref[0] * scale + shift
    nxt = xn_ref[0] * scale + shift
    h_ref[0:HALO] = jnp.where(ti > 0, prev, 0.0).astype(BF16)
    h_ref[HALO + tt:] = jnp.where(ti < last, nxt, 0.0).astype(BF16)


def _rnn_in(x, sc, sh, w_in, conv_w, conv_b, wgate, bgate, ap, h0, tt, reverse):
    b, t, _ = x.shape
    nt = t // tt
    nb = t // HALO
    per = tt // HALO

    def tix(i):
        return (nt - 1 - i) if reverse else i

    main = pl.BlockSpec((1, tt, D_MODEL), lambda bi, i: (bi, tix(i), 0))
    prev = pl.BlockSpec((1, HALO, D_MODEL), lambda bi, i: (bi, jnp.maximum(tix(i) * per - 1, 0), 0))
    nxt = pl.BlockSpec((1, HALO, D_MODEL), lambda bi, i: (bi, jnp.minimum((tix(i) + 1) * per, nb - 1), 0))
    vec = pl.BlockSpec((1, 1, D_MODEL), lambda bi, i: (bi, 0, 0))
    return pl.pallas_call(
        functools.partial(_rnn_in_kernel, tt=tt, reverse=reverse),
        out_shape=(jax.ShapeDtypeStruct((b, t, D_MODEL), F32),
                   jax.ShapeDtypeStruct((b, t, D_MODEL), BF16),
                   jax.ShapeDtypeStruct((b, t, D_MODEL), F32)),
        grid=(b, nt),
        in_specs=[main, prev, nxt, vec, vec,
                  _const_spec(w_in.shape), _const_spec(conv_w.shape), _const_spec(conv_b.shape),
                  _const_spec(wgate.shape), _const_spec(bgate.shape), _const_spec((1, D_MODEL)), vec],
        out_specs=(main, main, main),
        scratch_shapes=[pltpu.VMEM((tt + 2 * HALO, D_MODEL), BF16),
                        pltpu.VMEM((tt, D_MODEL), F32),
                        pltpu.VMEM((tt, D_MODEL), F32),
                        pltpu.VMEM((1, D_MODEL), F32)],
        compiler_params=_cparams(2), name="rnn_in_rev" if reverse else "rnn_in_fwd",
    )(x, x, x, sc, sh, w_in, conv_w, conv_b, wgate, bgate, ap, h0)


def _rnn_out_kernel(xb_ref, y_ref, rf_ref, x_ref, gate_ref, wgate_ref, bgate_ref, ap_ref, h0_ref,
                    w_ref, lg_ref, lb_ref, o_ref, a_ref, b_ref, rb_ref, carry_ref, *, tt):
    i = pl.program_id(1)

    @pl.when(i == 0)
    def _():
        carry_ref[...] = h0_ref[0]

    _rg_coeffs(xb_ref[0], wgate_ref, bgate_ref, ap_ref, a_ref, b_ref, True)
    carry_ref[...] = _scan_tile(a_ref, b_ref, rb_ref, carry_ref[...], tt, True)
    r = rf_ref[0] + rb_ref[...]
    m = (r * jax.nn.gelu(y_ref[0].astype(F32), approximate=True)).astype(BF16)
    o = jnp.dot(m, w_ref[...], preferred_element_type=F32)
    y = ALPHA * x_ref[0] + gate_ref[0] * o
    o_ref[0] = _layer_norm(y, lg_ref[...], lb_ref[...])


def _rnn_out(xb, y, rf, x, gate, wgate, bgate, ap, h0, w_out, ln_g, ln_b, tt):
    b, t, _ = x.shape
    nt = t // tt
    main = pl.BlockSpec((1, tt, D_MODEL), lambda bi, i: (bi, nt - 1 - i, 0))
    vec = pl.BlockSpec((1, 1, D_MODEL), lambda bi, i: (bi, 0, 0))
    par = _const_spec((1, D_MODEL))
    return pl.pallas_call(
        functools.partial(_rnn_out_kernel, tt=tt),
        out_shape=jax.ShapeDtypeStruct((b, t, D_MODEL), F32),
        grid=(b, nt),
        in_specs=[main, main, main, main, vec,
                  _const_spec(wgate.shape), _const_spec(bgate.shape), par, vec,
                  _const_spec(w_out.shape), par, par],
        out_specs=main,
        scratch_shapes=[pltpu.VMEM((tt, D_MODEL), F32)] * 3 + [pltpu.VMEM((1, D_MODEL), F32)],
        compiler_params=_cparams(2), name="rnn_out",
    )(xb, y, rf, x, gate, wgate, bgate, ap, h0, w_out, ln_g, ln_b)


def _rope_tables(n_tok):
    t = jnp.arange(n_tok)
    row = (t // GRID_W).astype(F32)[:, None]
    col = (t % GRID_W).astype(F32)[:, None]
    nf = HD // 4
    inv = 1.0 / (ROPE_THETA ** (jnp.arange(nf, dtype=F32) / nf))
    ang = jnp.concatenate([row * inv, row * inv, col * inv, col * inv], -1)
    sign = jnp.where((jnp.arange(HD) % (HD // 2)) < nf, -1.0, 1.0)
    reps = D_A // HD
    return jnp.tile(jnp.cos(ang), (1, reps)), jnp.tile(jnp.sin(ang) * sign, (1, reps))


def _tile(n, pref):
    return pref if n % pref == 0 else n


def kernel(x, c, ctx, c_ctx, ada_w, ada_b, ln1_g, ln1_b, ln2_g, ln2_b, ffn_w_up, ffn_conv_w, ffn_conv_b, ffn_w_down, att_w_in, att_w_out, diff_lq1, diff_lk1, diff_lq2, diff_lk2, diff_subln_g, na_rpb, rnn_w_in, rnn_conv_w, rnn_conv_b, rg_a_param, rg_wa, rg_ba, rg_wx, rg_bx, rnn_w_out):
    bsz, t, _ = x.shape
    n_ctx = ctx.shape[1]
    assert t % GRID_W == 0 and t // GRID_W >= WIN_R and n_ctx % HALO == 0

    n_rows = ((bsz + 1 + 7) // 8) * 8
    cc = jnp.zeros((n_rows, D_MODEL), F32).at[:bsz].set(c).at[bsz].set(c_ctx)
    mod = _ada_mod(cc, ada_w, ada_b)

    def mods(l):
        lat = mod[l, :bsz].reshape(bsz, 1, 6, D_MODEL)
        cx = jnp.broadcast_to(mod[l, bsz].reshape(1, 1, 6, D_MODEL), (bsz, 1, 6, D_MODEL))
        return ([lat[:, :, k] for k in range(6)], [cx[:, :, k] for k in range(6)])

    tm = _tile(t, 512)
    tmc = _tile(n_ctx, 512)

    (sh1, sc1, g1, sh2, sc2, g2), (csh1, csc1, cg1, csh2, csc2, cg2) = mods(0)
    lambda_init = 0.8 - 0.6 * math.exp(-0.3 * 0)
    w_in = att_w_in[0].astype(BF16)
    w_out = att_w_out[0].astype(BF16)
    cos_t, sin_t = _rope_tables(t)
    qa, ka, va, qb, kb, vb = _proj_attn(x, sc1, sh1, w_in, cos_t, sin_t, True, tm)
    qa_c, ka_c, va_c, qb_c, kb_c, vb_c = _proj_attn(
        ctx, csc1, csh1, w_in, cos_t[:n_ctx], sin_t[:n_ctx], False, tmc)
    lp = jnp.stack([diff_lq1[0], diff_lk1[0], diff_lq2[0], diff_lk2[0]])
    sub_g = diff_subln_g[0].reshape(1, 2 * HD)
    oa = _diff_attn(lp, sub_g, qa, ka_c, va_c, ka, va, lambda_init, _tile(t, 1024), _tile(t, 512))
    ob = _na_attn(qb, kb, vb, kb_c, vb_c, _na_bias_table(na_rpb[0]))
    oa_c, ob_c = _ctx_attn(lp, sub_g, qa_c, ka_c, va_c, qb_c, kb_c, vb_c, lambda_init)
    l1g, l1b = ln1_g[0].reshape(1, D_MODEL), ln1_b[0].reshape(1, D_MODEL)
    l2g, l2b = ln2_g[0].reshape(1, D_MODEL), ln2_b[0].reshape(1, D_MODEL)
    x = _outproj_ln(oa, ob, w_out, x, g1, l1g, l1b, tm)
    ctx = _outproj_ln(oa_c, ob_c, w_out, ctx, cg1, l1g, l1b, tmc)
    fw = _prep_ffn(ffn_w_up[0], ffn_conv_w[0], ffn_conv_b[0], ffn_w_down[0])
    x = _conv_ffn_ln(x, sc2, sh2, g2, fw, l2g, l2b, tm)
    ctx = _conv_ffn_ln(ctx, csc2, csh2, cg2, fw, l2g, l2b, tmc)

    (sh1, sc1, g1, sh2, sc2, g2), (csh1, csc1, _, _, _, _) = mods(1)
    w_in = rnn_w_in[0].astype(BF16)
    w_out = rnn_w_out[0].astype(BF16)
    conv_w = jnp.broadcast_to(rnn_conv_w[0][:, None, :], (CONV_RNN, 8, D_MODEL))
    conv_b = jnp.broadcast_to(rnn_conv_b[0][None, :], (8, D_MODEL))
    wgate = [(0.5 * jnp.concatenate([rg_wa[0, d], rg_wx[0, d]], axis=-1)).astype(BF16) for d in range(2)]
    bgate = [0.5 * jnp.concatenate([rg_ba[0, d], rg_bx[0, d]], axis=-1).reshape(N_RG_BLOCKS, 1, 2 * RG_BW)
             for d in range(2)]
    ap = [rg_a_param[0, d].reshape(1, D_MODEL) for d in range(2)]
    zeros = jnp.zeros((bsz, 1, D_MODEL), F32)
    tt = _tile(t, 512)
    ttc = _tile(n_ctx, 512)
    rc_f, _, _ = _rnn_in(ctx, csc1, csh1, w_in, conv_w, conv_b, wgate[0], bgate[0], ap[0],
                         zeros, ttc, False)
    rc_b, _, _ = _rnn_in(ctx, csc1, csh1, w_in, conv_w, conv_b, wgate[1], bgate[1], ap[1],
                         zeros, ttc, True)
    h0_f = rc_f[:, n_ctx - 1:n_ctx]
    h0_b = rc_b[:, 0:1]
    r_f, y, xb = _rnn_in(x, sc1, sh1, w_in, conv_w, conv_b, wgate[0], bgate[0], ap[0],
                         h0_f, tt, False)
    l1g, l1b = ln1_g[1].reshape(1, D_MODEL), ln1_b[1].reshape(1, D_MODEL)
    l2g, l2b = ln2_g[1].reshape(1, D_MODEL), ln2_b[1].reshape(1, D_MODEL)
    x = _rnn_out(xb, y, r_f, x, g1, wgate[1], bgate[1], ap[1], h0_b, w_out, l1g, l1b, tt)
    fw = _prep_ffn(ffn_w_up[1], ffn_conv_w[1], ffn_conv_b[1], ffn_w_down[1])
    x = _conv_ffn_ln(x, sc2, sh2, g2, fw, l2g, l2b, tm)
    return x
```

```python
import functools
import math

import jax
import jax.numpy as jnp
from jax import lax
from jax.experimental import pallas as pl
from jax.experimental.pallas import tpu as pltpu

F32 = jnp.float32
BF16 = jnp.bfloat16

D_MODEL = 1024
DEPTH = 2
GRID_W = 64
HD = 64
D_A = D_MODEL // 2
D_B = D_MODEL - D_A
H_A = D_A // (2 * HD)
H_B = D_B // HD
WIN_R = 8
WIN_C = 16
ROPE_THETA = 10000.0
N_RG_BLOCKS = 8
RG_BW = D_MODEL // N_RG_BLOCKS
RG_C = 8.0
CONV_RNN = 4
D_FF = ((8 * D_MODEL // 3 + 127) // 128) * 128
ALPHA = (2 * DEPTH) ** 0.25
NORM_EPS = 1e-5
NEG_BIG = -1e30
Q_SCALE = HD ** -0.5 * math.log2(math.e)
GELU_C1 = math.sqrt(2.0 / math.pi)
GELU_C2 = GELU_C1 * 0.044715

LANES = 128
SUBLANES = 8
HALO = 16
DENOM_ROWS = 16
FF_CHUNK = 256
N_FF_CHUNKS = D_FF // FF_CHUNK
VMEM_LIMIT = 56 * 1024 * 1024

_NT = (((1,), (1,)), ((), ()))


def _cparams(n_axes):
    return pltpu.CompilerParams(dimension_semantics=("arbitrary",) * n_axes,
                                vmem_limit_bytes=VMEM_LIMIT)


def _layer_norm(y, g, b):
    mu = jnp.mean(y, axis=-1, keepdims=True)
    yc = y - mu
    var = jnp.mean(yc * yc, axis=-1, keepdims=True)
    return yc * lax.rsqrt(var + NORM_EPS) * g + b


def _softmax_parts(s_list):
    m = s_list[0].max(axis=-1, keepdims=True)
    for s in s_list[1:]:
        m = jnp.maximum(m, s.max(axis=-1, keepdims=True))
    ps = [jnp.exp2(s - m) for s in s_list]
    l = ps[0].sum(axis=-1, keepdims=True)
    for p in ps[1:]:
        l = l + p.sum(axis=-1, keepdims=True)
    return ps, 1.0 / l


def _half_masks(dtype):
    lane = lax.broadcasted_iota(jnp.int32, (1, LANES), 1)
    lo = (lane < HD).astype(dtype)
    return lo, (1 - lo).astype(dtype)


def _ada_kernel(c_ref, w_ref, b_ref, o_ref):
    c = c_ref[...]
    a = c * jax.nn.sigmoid(c)
    o_ref[0] = jnp.dot(a, w_ref[0], preferred_element_type=F32,
                       precision=lax.Precision.HIGHEST) + b_ref[0]


def _ada_mod(cc, ada_w, ada_b):
    n = cc.shape[0]
    tn = 6 * D_MODEL // 4
    return pl.pallas_call(
        _ada_kernel,
        out_shape=jax.ShapeDtypeStruct((DEPTH, n, 6 * D_MODEL), F32),
        grid=(DEPTH, 6 * D_MODEL // tn),
        in_specs=[pl.BlockSpec((n, D_MODEL), lambda l, j: (0, 0)),
                  pl.BlockSpec((1, D_MODEL, tn), lambda l, j: (l, 0, j)),
                  pl.BlockSpec((1, 1, tn), lambda l, j: (l, 0, j))],
        out_specs=pl.BlockSpec((1, n, tn), lambda l, j: (l, 0, j)),
        compiler_params=_cparams(2), name="ada_mod",
    )(cc, ada_w, ada_b.reshape(DEPTH, 1, 6 * D_MODEL))


def _proj_attn_kernel(x_ref, sc_ref, sh_ref, w_ref, cos_ref, sin_ref,
                      qa_ref, ka_ref, va_ref, qb_ref, kb_ref, vb_ref, vt_ref, *, rope):
    h = (x_ref[0] * (1.0 + sc_ref[0]) + sh_ref[0]).astype(BF16)
    outs = (qa_ref, ka_ref, va_ref, qb_ref, kb_ref, vb_ref)
    for j, o_ref in enumerate(outs):
        r = jnp.dot(h, w_ref[:, j * D_A:(j + 1) * D_A], preferred_element_type=F32)
        if rope and j < 2:
            lane = lax.broadcasted_iota(jnp.int32, r.shape, 1)
            first_half = (lane % (HD // 2)) < (HD // 4)
            rot = jnp.where(first_half, pltpu.roll(r, D_A - HD // 4, 1), pltpu.roll(r, HD // 4, 1))
            r = r * cos_ref[...] + rot * sin_ref[...]
        if j in (0, 3):
            r = r * Q_SCALE
        if j in (2, 5):
            vt_ref[...] = r
            o_ref[0] = vt_ref[...].T.astype(BF16)
        else:
            o_ref[0] = r.astype(BF16)


def _proj_attn(x, sc, sh, w, cos_t, sin_t, rope, tm):
    b, t, _ = x.shape
    out = jax.ShapeDtypeStruct((b, t, D_A), BF16)
    out_t = jax.ShapeDtypeStruct((b, D_A, t), BF16)
    vec = pl.BlockSpec((1, 1, D_MODEL), lambda bi, i: (bi, 0, 0))
    blk = pl.BlockSpec((1, tm, D_A), lambda bi, i: (bi, i, 0))
    blk_t = pl.BlockSpec((1, D_A, tm), lambda bi, i: (bi, 0, i))
    tab = pl.BlockSpec((tm, D_A), lambda bi, i: (i, 0))
    return pl.pallas_call(
        functools.partial(_proj_attn_kernel, rope=rope),
        out_shape=(out, out, out_t, out, out, out_t),
        grid=(b, t // tm),
        in_specs=[pl.BlockSpec((1, tm, D_MODEL), lambda bi, i: (bi, i, 0)), vec, vec,
                  pl.BlockSpec((D_MODEL, 6 * D_A), lambda bi, i: (0, 0)), tab, tab],
        out_specs=(blk, blk, blk_t, blk, blk, blk_t),
        scratch_shapes=[pltpu.VMEM((tm, D_A), F32)],
        compiler_params=_cparams(2), name="proj_attn",
    )(x, sc, sh, w, cos_t, sin_t)


def _lambda_full(lp_ref, lambda_init):
    lp = lp_ref[...]
    s1 = jnp.sum(lp[0:1] * lp[1:2], axis=-1, keepdims=True)
    s2 = jnp.sum(lp[2:3] * lp[3:4], axis=-1, keepdims=True)
    return jnp.exp(s1) - jnp.exp(s2) + lambda_init


def _head_rms(o, g, lambda_init):
    o = o * lax.rsqrt(jnp.mean(o * o, axis=-1, keepdims=True) + NORM_EPS)
    return o * g * (1.0 - lambda_init)


def _diff_attn_kernel(lp_ref, g_ref, q_ref, kc_ref, vct_ref, kl_ref, vlt_ref, o_ref,
                      s_ref, p_ref, acc_ref, *, tk, n_chunks, n_ctx, lambda_init):
    lam = _lambda_full(lp_ref, lambda_init)
    q = q_ref[0]
    lo, hi = _half_masks(BF16)
    qs = (q * lo, q * hi)
    tq = q.shape[0]
    chunks = [(kc_ref, vct_ref, 0, n_ctx)] + [(kl_ref, vlt_ref, j * tk, tk) for j in range(n_chunks)]
    n_all = len(chunks)
    acc_ref[...] = jnp.zeros_like(acc_ref)

    def scores(c):
        k_ref, _, off, size = chunks[c]
        k = k_ref[0, off:off + size, :]
        for mi in range(2):
            s_ref[c % 2, mi, :size, :] = lax.dot_general(k, qs[mi], _NT, preferred_element_type=F32)

    def softmax(c, m_old):
        size = chunks[c][3]
        m_out, a_out = [], []
        for mi in range(2):
            s = s_ref[c % 2, mi, :size, :]
            m_new = jnp.maximum(m_old[mi], s.max(axis=0, keepdims=True))
            a_out.append(jnp.exp2(m_old[mi] - m_new))
            p_ref[c % 2, mi, :size, :] = jnp.exp2(s - m_new).astype(BF16)
            m_out.append(m_new)
        return m_out, a_out

    def values(c, a):
        _, vt_ref, off, size = chunks[c]
        vext = jnp.concatenate([vt_ref[0, :, off:off + size], jnp.ones((DENOM_ROWS, size), BF16)], axis=0)
        for mi in range(2):
            pv = jnp.dot(vext, p_ref[c % 2, mi, :size, :], preferred_element_type=F32)
            acc_ref[mi] = a[mi] * acc_ref[mi] + pv

    m = [jnp.full((1, tq), NEG_BIG, F32)] * 2
    a_next = None
    for i in range(-2, n_all):
        a_cur = a_next
        if i + 2 < n_all:
            scores(i + 2)
        if 0 <= i + 1 < n_all:
            m, a_next = softmax(i + 1, m)
        if i >= 0:
            values(i, a_cur)

    outs = []
    for mi in range(2):
        acc = acc_ref[mi]
        outs.append(acc[:2 * HD] / acc[2 * HD:2 * HD + 1])
    o = (outs[0] - lam * outs[1]).T
    o_ref[0] = _head_rms(o, g_ref[...], lambda_init).astype(BF16)


def _diff_attn(lp, g, qa, ka_c, vat_c, ka, vat, lambda_init, tq, tk):
    b, t, _ = qa.shape
    n_ctx = ka_c.shape[1]
    assert n_ctx <= tk and n_ctx % LANES == 0
    q_blk = pl.BlockSpec((1, tq, LANES), lambda bi, h, i: (bi, i, h))
    return pl.pallas_call(
        functools.partial(_diff_attn_kernel, tk=tk, n_chunks=t // tk, n_ctx=n_ctx, lambda_init=lambda_init),
        out_shape=jax.ShapeDtypeStruct((b, t, D_A), BF16),
        grid=(b, H_A, t // tq),
        in_specs=[pl.BlockSpec((4, HD), lambda bi, h, i: (0, 0)),
                  pl.BlockSpec((1, 2 * HD), lambda bi, h, i: (0, 0)),
                  q_blk,
                  pl.BlockSpec((1, n_ctx, LANES), lambda bi, h, i: (bi, 0, h)),
                  pl.BlockSpec((1, LANES, n_ctx), lambda bi, h, i: (bi, h, 0)),
                  pl.BlockSpec((1, t, LANES), lambda bi, h, i: (bi, 0, h)),
                  pl.BlockSpec((1, LANES, t), lambda bi, h, i: (bi, h, 0))],
        out_specs=q_blk,
        scratch_shapes=[pltpu.VMEM((2, 2, tk, tq), F32),
                        pltpu.VMEM((2, 2, tk, tq), BF16),
                        pltpu.VMEM((2, 2 * HD + DENOM_ROWS, tq), F32)],
        compiler_params=_cparams(3), name="diff_attn",
    )(lp, g, qa, ka_c, vat_c, ka, vat)


NA_QROWS = 4
NA_KROWS = NA_QROWS + WIN_R
NA_REL_PAD = NA_QROWS
NA_REL_N = NA_KROWS + WIN_R + NA_QROWS - 2


def _na_kernel(q_ref, k_ref, vt_ref, kc_ref, vct_ref, tab_ref, o_ref, s_ref, p_ref, ot_ref, *, rows, n_ctx):
    g = pl.program_id(1)
    r0 = g * NA_QROWS
    ws = jnp.clip(r0 - WIN_R // 2, 0, rows - NA_KROWS)
    d0 = r0 - ws
    k_start = pl.multiple_of(ws * GRID_W, NA_QROWS * GRID_W)
    n_loc = NA_KROWS * GRID_W
    nq = NA_QROWS * GRID_W
    lo, hi = _half_masks(BF16)

    qr = lax.broadcasted_iota(jnp.int32, (1, nq), 1) // GRID_W
    rs = jnp.clip(r0 + qr - WIN_R // 2, 0, rows - WIN_R)
    row_mask = []
    for kr in range(NA_KROWS):
        valid = (ws + kr >= rs) & (ws + kr < rs + WIN_R)
        row_mask.append(jnp.where(valid, 0.0, NEG_BIG))

    def scores(h):
        hp, e = divmod(h, 2)
        cols = slice(hp * LANES, (hp + 1) * LANES)
        qm = q_ref[0, :, cols] * (lo, hi)[e]
        s_loc = lax.dot_general(k_ref[0, pl.ds(k_start, n_loc), cols], qm, _NT, preferred_element_type=F32)
        for kr in range(NA_KROWS):
            base = kr - d0 + (WIN_R - 1) + (NA_REL_PAD - 1)
            bias = jnp.concatenate([tab_ref[h, base - 2 * p] for p in range(NA_QROWS // 2)], axis=1)
            rows_kr = slice(kr * GRID_W, (kr + 1) * GRID_W)
            s_ref[h % 2, rows_kr, :] = s_loc[rows_kr] + bias + row_mask[kr]
        s_ref[h % 2, n_loc:, :] = lax.dot_general(kc_ref[0, :, cols], qm, _NT, preferred_element_type=F32)

    def softmax(h):
        s = s_ref[h % 2]
        m = s.max(axis=0, keepdims=True)
        p_ref[h % 2] = jnp.exp2(s - m).astype(BF16)

    def values(h):
        hrows = slice(h * HD, (h + 1) * HD)
        vt = jnp.concatenate([vt_ref[0, hrows, pl.ds(k_start, n_loc)], vct_ref[0, hrows, :]], axis=1)
        vext = jnp.concatenate([vt, jnp.ones((DENOM_ROWS, n_loc + n_ctx), BF16)], axis=0)
        o = jnp.dot(vext, p_ref[h % 2], preferred_element_type=F32)
        ot_ref[hrows, :] = o[:HD] / o[HD:HD + 1]

    for i in range(-2, H_B):
        if i + 2 < H_B:
            scores(i + 2)
        if 0 <= i + 1 < H_B:
            softmax(i + 1)
        if i >= 0:
            values(i)
    o_ref[0] = ot_ref[...].T.astype(BF16)


def _na_bias_table(rpb):
    c = jnp.arange(GRID_W)[None, :]
    kc = jnp.arange(GRID_W)[:, None]
    cstart = jnp.clip(c - WIN_C // 2, 0, GRID_W - WIN_C)
    inside = (kc >= cstart) & (kc < cstart + WIN_C)
    rel_c = jnp.clip(kc - c + (WIN_C - 1), 0, 2 * WIN_C - 2)
    band = jnp.where(inside, rpb[:, :, rel_c] * math.log2(math.e), NEG_BIG)
    hi_pad = NA_REL_N - NA_REL_PAD + 1 - (2 * WIN_R - 1)
    band = jnp.pad(band, ((0, 0), (NA_REL_PAD, hi_pad), (0, 0), (0, 0)), constant_values=NEG_BIG)
    return jnp.concatenate([band[:, 1:], band[:, :-1]], axis=-1).astype(F32)


def _na_attn(qb, kb, vbt, kb_c, vbt_c, tab):
    b, t, _ = qb.shape
    rows = t // GRID_W
    n_ctx = kb_c.shape[1]
    assert rows % NA_QROWS == 0 and rows >= NA_KROWS and tab.shape[1] == NA_REL_N
    nq = NA_QROWS * GRID_W
    n_keys = NA_KROWS * GRID_W + n_ctx
    qblk = pl.BlockSpec((1, nq, D_B), lambda bi, g: (bi, g, 0))
    return pl.pallas_call(
        functools.partial(_na_kernel, rows=rows, n_ctx=n_ctx),
        out_shape=jax.ShapeDtypeStruct((b, t, D_B), BF16),
        grid=(b, rows // NA_QROWS),
        in_specs=[qblk,
                  pl.BlockSpec((1, t, D_B), lambda bi, g: (bi, 0, 0)),
                  pl.BlockSpec((1, D_B, t), lambda bi, g: (bi, 0, 0)),
                  pl.BlockSpec((1, n_ctx, D_B), lambda bi, g: (bi, 0, 0)),
                  pl.BlockSpec((1, D_B, n_ctx), lambda bi, g: (bi, 0, 0)),
                  pl.BlockSpec(tab.shape, lambda bi, g: (0, 0, 0, 0))],
        out_specs=qblk,
        scratch_shapes=[pltpu.VMEM((2, n_keys, nq), F32),
                        pltpu.VMEM((2, n_keys, nq), BF16),
                        pltpu.VMEM((D_B, nq), F32)],
        compiler_params=_cparams(2), name="na_attn",
    )(qb, kb, vbt, kb_c, vbt_c, tab)


def _ctx_attn_kernel(lp_ref, g_ref, qa_ref, ka_ref, vat_ref, qb_ref, kb_ref, vbt_ref,
                     oa_ref, ob_ref, *, lambda_init):
    lam = _lambda_full(lp_ref, lambda_init)
    lo, hi = _half_masks(BF16)
    lane = lax.broadcasted_iota(jnp.int32, (1, LANES), 1)
    for h in range(H_A):
        cols = slice(h * LANES, (h + 1) * LANES)
        q, k, vt = qa_ref[0, :, cols], ka_ref[0, :, cols], vat_ref[0, cols, :]
        probs = []
        for msk in (lo, hi):
            s = lax.dot_general(q * msk, k, _NT, preferred_element_type=F32)
            (p,), inv = _softmax_parts([s])
            probs.append(p * inv)
        w = (probs[0] - lam * probs[1]).astype(BF16)
        o = lax.dot_general(w, vt, _NT, preferred_element_type=F32)
        oa_ref[0, :, cols] = _head_rms(o, g_ref[...], lambda_init).astype(BF16)
    for hp in range(H_B // 2):
        cols = slice(hp * LANES, (hp + 1) * LANES)
        q, k, vt = qb_ref[0, :, cols], kb_ref[0, :, cols], vbt_ref[0, cols, :]
        halves = []
        for msk in (lo, hi):
            s = lax.dot_general(q * msk, k, _NT, preferred_element_type=F32)
            (p,), inv = _softmax_parts([s])
            halves.append(lax.dot_general((p * inv).astype(BF16), vt, _NT, preferred_element_type=F32))
        ob_ref[0, :, cols] = jnp.where(lane < HD, halves[0], halves[1]).astype(BF16)


def _ctx_attn(lp, g, qa, ka, vat, qb, kb, vbt, lambda_init):
    b, n, _ = qa.shape
    blk = pl.BlockSpec((1, n, D_A), lambda bi: (bi, 0, 0))
    blk_t = pl.BlockSpec((1, D_A, n), lambda bi: (bi, 0, 0))
    out = jax.ShapeDtypeStruct((b, n, D_A), BF16)
    return pl.pallas_call(
        functools.partial(_ctx_attn_kernel, lambda_init=lambda_init),
        out_shape=(out, out),
        grid=(b,),
        in_specs=[pl.BlockSpec((4, HD), lambda bi: (0, 0)),
                  pl.BlockSpec((1, 2 * HD), lambda bi: (0, 0)),
                  blk, blk, blk_t, blk, blk, blk_t],
        out_specs=(blk, blk),
        compiler_params=_cparams(1), name="ctx_attn",
    )(lp, g, qa, ka, vat, qb, kb, vbt)


def _outproj_ln_kernel(oa_ref, ob_ref, w_ref, x_ref, gate_ref, g_ref, b_ref, o_ref):
    o = (jnp.dot(oa_ref[0], w_ref[:D_A], preferred_element_type=F32)
         + jnp.dot(ob_ref[0], w_ref[D_A:], preferred_element_type=F32))
    y = ALPHA * x_ref[0] + gate_ref[0] * o
    o_ref[0] = _layer_norm(y, g_ref[...], b_ref[...])


def _outproj_ln(oa, ob, w, x, gate, ln_g, ln_b, tm):
    b, t, _ = x.shape
    half = pl.BlockSpec((1, tm, D_A), lambda bi, i: (bi, i, 0))
    full = pl.BlockSpec((1, tm, D_MODEL), lambda bi, i: (bi, i, 0))
    vec = pl.BlockSpec((1, 1, D_MODEL), lambda bi, i: (bi, 0, 0))
    par = pl.BlockSpec((1, D_MODEL), lambda bi, i: (0, 0))
    return pl.pallas_call(
        _outproj_ln_kernel,
        out_shape=jax.ShapeDtypeStruct((b, t, D_MODEL), F32),
        grid=(b, t // tm),
        in_specs=[half, half, pl.BlockSpec((D_MODEL, D_MODEL), lambda bi, i: (0, 0)),
                  full, vec, par, par],
        out_specs=full,
        compiler_params=_cparams(2), name="outproj_ln",
    )(oa, ob, w, x, gate, ln_g, ln_b)


def _fill_modulated(h_ref, x_ref, xp_ref, xn_ref, sc_ref, sh_ref, tm, reverse=False):
    i = pl.program_id(1)
    last = pl.num_programs(1) - 1
    ti = (last - i) if reverse else i
    scale = 1.0 + sc_ref[0]
    shift = sh_ref[0]
    h_ref[HALO:HALO + tm] = (x_ref[0] * scale + shift).astype(BF16)
    prev = xp_ref[0] * scale + shift
    nxt = xn_ref[0] * scale + shift
    h_ref[0:HALO] = jnp.where(ti > 0, prev, 0.0).astype(BF16)
    h_ref[HALO + tm:] = jnp.where(ti < last, nxt, 0.0).astype(BF16)


def _conv_ffn_kernel(x_ref, xp_ref, xn_ref, sc_ref, sh_ref, gate_ref,
                     wg_ref, wv_ref, cwg_ref, cbg_ref, cwv_ref, cbv_ref, wd_ref,
                     lg_ref, lb_ref, o_ref, h_ref, ug_ref, uv_ref, a_ref, *, tm):
    _fill_modulated(h_ref, x_ref, xp_ref, xn_ref, sc_ref, sh_ref, tm)

    def conv3(u_ref, slot, cw, cb):
        u = u_ref[slot]
        rows, cols = u.shape
        u3 = u.reshape(rows // SUBLANES, SUBLANES, cols)
        sub = lax.broadcasted_iota(jnp.int32, u3.shape, 1)
        dn = pltpu.roll(u3, 1, 1)
        dn = jnp.where(sub == 0, jnp.concatenate([dn[-1:], dn[:-1]], axis=0), dn)
        up_ = pltpu.roll(u3, SUBLANES - 1, 1)
        up_ = jnp.where(sub == SUBLANES - 1, jnp.concatenate([up_[1:], up_[:1]], axis=0), up_)
        full = cw[0][None] * dn + cw[1][None] * u3 + cw[2][None] * up_ + cb[0][None]
        return full.reshape(rows, cols)[HALO:HALO + tm]

    def up(c, slot):
        h = h_ref[...]
        ug_ref[slot] = jnp.dot(h, wg_ref[c], preferred_element_type=F32)
        uv_ref[slot] = jnp.dot(h, wv_ref[c], preferred_element_type=F32)

    def act(c, slot):
        g = conv3(ug_ref, slot, cwg_ref[c], cbg_ref[c])
        v = conv3(uv_ref, slot, cwv_ref[c], cbv_ref[c])
        col = c * FF_CHUNK
        if not isinstance(c, int):
            col = pl.multiple_of(col, FF_CHUNK)
        t = jnp.tanh(g * (GELU_C1 + GELU_C2 * (g * g)))
        a_ref[:, pl.ds(col, FF_CHUNK)] = ((g + g * t) * v).astype(BF16)

    n = N_FF_CHUNKS
    up(0, 0)
    n_pairs = 0

    def body(j, _):
        t = 1 + 2 * j
        up(t, 1)
        act(t - 1, 0)
        up(t + 1, 0)
        act(t, 1)
        return 0

    lax.fori_loop(0, n_pairs, body, 0)
    for t in range(1 + 2 * n_pairs, n):
        up(t, t % 2)
        act(t - 1, (t - 1) % 2)
    act(n - 1, (n - 1) % 2)
    o = jnp.dot(a_ref[...], wd_ref[...], preferred_element_type=F32)
    y = ALPHA * x_ref[0] + gate_ref[0] * o
    o_ref[0] = _layer_norm(y, lg_ref[...], lb_ref[...])


def _halo_specs(t, tm):
    nb = t // HALO
    per = tm // HALO
    main = pl.BlockSpec((1, tm, D_MODEL), lambda bi, i: (bi, i, 0))
    prev = pl.BlockSpec((1, HALO, D_MODEL), lambda bi, i: (bi, jnp.maximum(i * per - 1, 0), 0))
    nxt = pl.BlockSpec((1, HALO, D_MODEL), lambda bi, i: (bi, jnp.minimum((i + 1) * per, nb - 1), 0))
    return main, prev, nxt


def _const_spec(shape):
    nd = len(shape)
    return pl.BlockSpec(shape, lambda bi, i: (0,) * nd, pipeline_mode=pl.Buffered(1))


def _conv_ffn_ln(x, sc, sh, gate, fw, ln_g, ln_b, tm):
    b, t, _ = x.shape
    main, prev, nxt = _halo_specs(t, tm)
    vec = pl.BlockSpec((1, 1, D_MODEL), lambda bi, i: (bi, 0, 0))
    par = _const_spec((1, D_MODEL))
    wg, wv, cwg, cbg, cwv, cbv, wd = fw
    return pl.pallas_call(
        functools.partial(_conv_ffn_kernel, tm=tm),
        out_shape=jax.ShapeDtypeStruct((b, t, D_MODEL), F32),
        grid=(b, t // tm),
        in_specs=[main, prev, nxt, vec, vec, vec,
                  _const_spec(wg.shape), _const_spec(wv.shape),
                  _const_spec(cwg.shape), _const_spec(cbg.shape),
                  _const_spec(cwv.shape), _const_spec(cbv.shape),
                  _const_spec(wd.shape), par, par],
        out_specs=main,
        scratch_shapes=[pltpu.VMEM((tm + 2 * HALO, D_MODEL), BF16),
                        pltpu.VMEM((2, tm + 2 * HALO, FF_CHUNK), F32),
                        pltpu.VMEM((2, tm + 2 * HALO, FF_CHUNK), F32),
                        pltpu.VMEM((tm, D_FF), BF16)],
        compiler_params=_cparams(2), name="conv_ffn_ln",
    )(x, x, x, sc, sh, gate, wg, wv, cwg, cbg, cwv, cbv, wd, ln_g, ln_b)


def _prep_ffn(w_up, conv_w, conv_b, w_down):
    def chunks_cols(a):
        return jnp.transpose(a.reshape(a.shape[0], N_FF_CHUNKS, FF_CHUNK), (1, 0, 2))
    wg = chunks_cols(w_up[:, :D_FF]).astype(BF16)
    wv = chunks_cols(w_up[:, D_FF:]).astype(BF16)
    def sublane_rep(a):
        return jnp.broadcast_to(a[:, :, None, :], a.shape[:2] + (SUBLANES, FF_CHUNK))
    cwg = sublane_rep(chunks_cols(conv_w[:, :D_FF]))
    cwv = sublane_rep(chunks_cols(0.5 * conv_w[:, D_FF:]))
    cbg = sublane_rep(chunks_cols(conv_b[None, :D_FF]))
    cbv = sublane_rep(chunks_cols(0.5 * conv_b[None, D_FF:]))
    wd = w_down.astype(BF16)
    return wg, wv, cwg, cbg, cwv, cbv, wd


def _rg_coeffs(xb, wgate_ref, bgate_ref, ap_ref, a_ref, b_ref, reverse):
    ap = ap_ref[...]
    neg = -ap
    sp = jnp.maximum(neg, 0.0) + jnp.log1p(jnp.exp(-jnp.abs(neg)))
    half_c = (-0.5 * RG_C) * sp
    for n in range(N_RG_BLOCKS):
        cols = slice(n * RG_BW, (n + 1) * RG_BW)
        xn = xb[:, cols]
        t = jnp.tanh(jnp.dot(xn.astype(BF16), wgate_ref[n], preferred_element_type=F32) + bgate_ref[n])
        log_a = half_c[:, cols] + half_c[:, cols] * t[:, :RG_BW]
        gate_x = 0.5 + 0.5 * t[:, RG_BW:]
        a = jnp.exp(log_a)
        one_minus_a2 = jnp.tanh(log_a) * (-1.0 - a * a)
        root = jnp.where(one_minus_a2 > 0.0, one_minus_a2 * lax.rsqrt(one_minus_a2), 0.0)
        a_ref[:, cols], b_ref[:, cols] = _group_prefix(a, gate_x * xn * root, reverse)


def _group_prefix(a, b, reverse):
    rows, cols = a.shape
    a = a.reshape(rows // SUBLANES, SUBLANES, cols)
    b = b.reshape(rows // SUBLANES, SUBLANES, cols)
    sub = lax.broadcasted_iota(jnp.int32, a.shape, 1)
    for k in (1, 2, 4):
        shift = (SUBLANES - k) if reverse else k
        msk = (sub < SUBLANES - k) if reverse else (sub >= k)
        a_sh = pltpu.roll(a, shift, 1)
        b_sh = pltpu.roll(b, shift, 1)
        b = jnp.where(msk, a * b_sh + b, b)
        a = jnp.where(msk, a * a_sh, a)
    return a.reshape(rows, cols), b.reshape(rows, cols)


def _scan_tile(a_ref, b_ref, r_ref, h_in, tt, reverse):
    n_groups = tt // SUBLANES
    h = h_in
    for gi in range(n_groups):
        g = (n_groups - 1 - gi) if reverse else gi
        rows = slice(g * SUBLANES, (g + 1) * SUBLANES)
        hg = a_ref[rows, :] * h + b_ref[rows, :]
        r_ref[rows, :] = hg
        h = hg[0:1] if reverse else hg[SUBLANES - 1:SUBLANES]
    return h


def _rnn_in_kernel(x_ref, xp_ref, xn_ref, sc_ref, sh_ref, w_ref, cw_ref, cb_ref,
                   wgate_ref, bgate_ref, ap_ref, h0_ref,
                   r_ref, y_ref, xb_ref, h_ref, a_ref, b_ref, carry_ref, *, tt, reverse):
    i = pl.program_id(1)

    @pl.when(i == 0)
    def _():
        carry_ref[...] = h0_ref[0]

    _fill_modulated(h_ref, x_ref, xp_ref, xn_ref, sc_ref, sh_ref, tt, reverse)
    h = h_ref[...]
    y_ref[0] = jnp.dot(h[HALO:HALO + tt], w_ref[:, :D_MODEL], preferred_element_type=F32).astype(BF16)
    u = jnp.dot(h, w_ref[:, D_MODEL:], preferred_element_type=F32)
    u3 = u.reshape((tt + 2 * HALO) // SUBLANES, SUBLANES, D_MODEL)
    sub = lax.broadcasted_iota(jnp.int32, u3.shape, 1)

    def shifted(k):
        rolled = pltpu.roll(u3, k % SUBLANES, 1)
        if k > 0:
            return jnp.where(sub < k, jnp.concatenate([rolled[-1:], rolled[:-1]], axis=0), rolled)
        return jnp.where(sub >= SUBLANES + k, jnp.concatenate([rolled[1:], rolled[:1]], axis=0), rolled)

    full = (cw_ref[0][None] * shifted(2) + cw_ref[1][None] * shifted(1) + cw_ref[2][None] * u3
            + cw_ref[3][None] * shifted(-1) + cb_ref[...][None])
    xb = full.reshape(tt + 2 * HALO, D_MODEL)[HALO:HALO + tt]
    xb_ref[0] = xb
    _rg_coeffs(xb, wgate_ref, bgate_ref, ap_ref, a_ref, b_ref, reverse)
    carry_ref[...] = _scan_tile(a_ref, b_ref, r_ref.at[0], carry_ref[...], tt, reverse)


def _rnn_in(x, sc, sh, w_in, conv_w, conv_b, wgate, bgate, ap, h0, tt, reverse):
    b, t, _ = x.shape
    nt = t // tt
    nb = t // HALO
    per = tt // HALO

    def tix(i):
        return (nt - 1 - i) if reverse else i

    main = pl.BlockSpec((1, tt, D_MODEL), lambda bi, i: (bi, tix(i), 0))
    prev = pl.BlockSpec((1, HALO, D_MODEL), lambda bi, i: (bi, jnp.maximum(tix(i) * per - 1, 0), 0))
    nxt = pl.BlockSpec((1, HALO, D_MODEL), lambda bi, i: (bi, jnp.minimum((tix(i) + 1) * per, nb - 1), 0))
    vec = pl.BlockSpec((1, 1, D_MODEL), lambda bi, i: (bi, 0, 0))
    return pl.pallas_call(
        functools.partial(_rnn_in_kernel, tt=tt, reverse=reverse),
        out_shape=(jax.ShapeDtypeStruct((b, t, D_MODEL), F32),
                   jax.ShapeDtypeStruct((b, t, D_MODEL), BF16),
                   jax.ShapeDtypeStruct((b, t, D_MODEL), F32)),
        grid=(b, nt),
        in_specs=[main, prev, nxt, vec, vec,
                  _const_spec(w_in.shape), _const_spec(conv_w.shape), _const_spec(conv_b.shape),
                  _const_spec(wgate.shape), _const_spec(bgate.shape), _const_spec((1, D_MODEL)), vec],
        out_specs=(main, main, main),
        scratch_shapes=[pltpu.VMEM((tt + 2 * HALO, D_MODEL), BF16),
                        pltpu.VMEM((tt, D_MODEL), F32),
                        pltpu.VMEM((tt, D_MODEL), F32),
                        pltpu.VMEM((1, D_MODEL), F32)],
        compiler_params=_cparams(2), name="rnn_in_rev" if reverse else "rnn_in_fwd",
    )(x, x, x, sc, sh, w_in, conv_w, conv_b, wgate, bgate, ap, h0)


def _rnn_out_kernel(xb_ref, y_ref, rf_ref, x_ref, gate_ref, wgate_ref, bgate_ref, ap_ref, h0_ref,
                    w_ref, lg_ref, lb_ref, o_ref, a_ref, b_ref, rb_ref, carry_ref, *, tt):
    i = pl.program_id(1)

    @pl.when(i == 0)
    def _():
        carry_ref[...] = h0_ref[0]

    _rg_coeffs(xb_ref[0], wgate_ref, bgate_ref, ap_ref, a_ref, b_ref, True)
    carry_ref[...] = _scan_tile(a_ref, b_ref, rb_ref, carry_ref[...], tt, True)
    r = rf_ref[0] + rb_ref[...]
    yg = y_ref[0].astype(F32)
    hy = 0.5 * yg
    m = (r * (hy + hy * jnp.tanh(yg * (GELU_C1 + GELU_C2 * (yg * yg))))).astype(BF16)
    o = jnp.dot(m, w_ref[...], preferred_element_type=F32)
    y = ALPHA * x_ref[0] + gate_ref[0] * o
    o_ref[0] = _layer_norm(y, lg_ref[...], lb_ref[...])


def _rnn_out(xb, y, rf, x, gate, wgate, bgate, ap, h0, w_out, ln_g, ln_b, tt):
    b, t, _ = x.shape
    nt = t // tt
    main = pl.BlockSpec((1, tt, D_MODEL), lambda bi, i: (bi, nt - 1 - i, 0))
    vec = pl.BlockSpec((1, 1, D_MODEL), lambda bi, i: (bi, 0, 0))
    par = _const_spec((1, D_MODEL))
    return pl.pallas_call(
        functools.partial(_rnn_out_kernel, tt=tt),
        out_shape=jax.ShapeDtypeStruct((b, t, D_MODEL), F32),
        grid=(b, nt),
        in_specs=[main, main, main, main, vec,
                  _const_spec(wgate.shape), _const_spec(bgate.shape), par, vec,
                  _const_spec(w_out.shape), par, par],
        out_specs=main,
        scratch_shapes=[pltpu.VMEM((tt, D_MODEL), F32)] * 3 + [pltpu.VMEM((1, D_MODEL), F32)],
        compiler_params=_cparams(2), name="rnn_out",
    )(xb, y, rf, x, gate, wgate, bgate, ap, h0, w_out, ln_g, ln_b)


def _rope_tables(n_tok):
    t = jnp.arange(n_tok)
    row = (t // GRID_W).astype(F32)[:, None]
    col = (t % GRID_W).astype(F32)[:, None]
    nf = HD // 4
    inv = 1.0 / (ROPE_THETA ** (jnp.arange(nf, dtype=F32) / nf))
    ang = jnp.concatenate([row * inv, row * inv, col * inv, col * inv], -1)
    sign = jnp.where((jnp.arange(HD) % (HD // 2)) < nf, -1.0, 1.0)
    reps = D_A // HD
    return jnp.tile(jnp.cos(ang), (1, reps)), jnp.tile(jnp.sin(ang) * sign, (1, reps))


ROW_TILE = 512
ATTN_Q_TILE = 1024
ATTN_K_CHUNK = 512


def _tile(n, pref):
    return pref if n % pref == 0 else n


def kernel(x, c, ctx, c_ctx, ada_w, ada_b, ln1_g, ln1_b, ln2_g, ln2_b, ffn_w_up, ffn_conv_w, ffn_conv_b, ffn_w_down, att_w_in, att_w_out, diff_lq1, diff_lk1, diff_lq2, diff_lk2, diff_subln_g, na_rpb, rnn_w_in, rnn_conv_w, rnn_conv_b, rg_a_param, rg_wa, rg_ba, rg_wx, rg_bx, rnn_w_out):
    bsz, t, _ = x.shape
    n_ctx = ctx.shape[1]
    assert t % GRID_W == 0 and t // GRID_W >= WIN_R and n_ctx % HALO == 0

    n_rows = -(-(bsz + 1) // SUBLANES) * SUBLANES
    cc = jnp.zeros((n_rows, D_MODEL), F32).at[:bsz].set(c).at[bsz].set(c_ctx)
    mod = _ada_mod(cc, ada_w, ada_b)

    def mods(l):
        lat = mod[l, :bsz].reshape(bsz, 1, 6, D_MODEL)
        cx = jnp.broadcast_to(mod[l, bsz].reshape(1, 1, 6, D_MODEL), (bsz, 1, 6, D_MODEL))
        return ([lat[:, :, k] for k in range(6)], [cx[:, :, k] for k in range(6)])

    tm = _tile(t, ROW_TILE)
    tmc = _tile(n_ctx, ROW_TILE)

    (sh1, sc1, g1, sh2, sc2, g2), (csh1, csc1, cg1, csh2, csc2, cg2) = mods(0)
    lambda_init = 0.8 - 0.6 * math.exp(-0.3 * 0)
    w_in = att_w_in[0].astype(BF16)
    w_out = att_w_out[0].astype(BF16)
    cos_t, sin_t = _rope_tables(t)
    qa, ka, vat, qb, kb, vbt = _proj_attn(x, sc1, sh1, w_in, cos_t, sin_t, True, tm)
    qa_c, ka_c, vat_c, qb_c, kb_c, vbt_c = _proj_attn(
        ctx, csc1, csh1, w_in, cos_t[:n_ctx], sin_t[:n_ctx], False, tmc)
    lp = jnp.stack([diff_lq1[0], diff_lk1[0], diff_lq2[0], diff_lk2[0]])
    sub_g = diff_subln_g[0].reshape(1, 2 * HD)
    oa = _diff_attn(lp, sub_g, qa, ka_c, vat_c, ka, vat, lambda_init,
                    _tile(t, ATTN_Q_TILE), _tile(t, ATTN_K_CHUNK))
    ob = _na_attn(qb, kb, vbt, kb_c, vbt_c, _na_bias_table(na_rpb[0]))
    oa_c, ob_c = _ctx_attn(lp, sub_g, qa_c, ka_c, vat_c, qb_c, kb_c, vbt_c, lambda_init)
    l1g, l1b = ln1_g[0].reshape(1, D_MODEL), ln1_b[0].reshape(1, D_MODEL)
    l2g, l2b = ln2_g[0].reshape(1, D_MODEL), ln2_b[0].reshape(1, D_MODEL)
    x = _outproj_ln(oa, ob, w_out, x, g1, l1g, l1b, tm)
    ctx = _outproj_ln(oa_c, ob_c, w_out, ctx, cg1, l1g, l1b, tmc)
    fw = _prep_ffn(ffn_w_up[0], ffn_conv_w[0], ffn_conv_b[0], ffn_w_down[0])
    x = _conv_ffn_ln(x, sc2, sh2, g2, fw, l2g, l2b, tm)
    ctx = _conv_ffn_ln(ctx, csc2, csh2, cg2, fw, l2g, l2b, tmc)

    (sh1, sc1, g1, sh2, sc2, g2), (csh1, csc1, _, _, _, _) = mods(1)
    w_in = rnn_w_in[0].astype(BF16)
    w_out = rnn_w_out[0].astype(BF16)
    conv_w = jnp.broadcast_to(rnn_conv_w[0][:, None, :], (CONV_RNN, SUBLANES, D_MODEL))
    conv_b = jnp.broadcast_to(rnn_conv_b[0][None, :], (SUBLANES, D_MODEL))
    wgate = [(0.5 * jnp.concatenate([rg_wa[0, d], rg_wx[0, d]], axis=-1)).astype(BF16) for d in range(2)]
    bgate = [0.5 * jnp.concatenate([rg_ba[0, d], rg_bx[0, d]], axis=-1).reshape(N_RG_BLOCKS, 1, 2 * RG_BW)
             for d in range(2)]
    ap = [rg_a_param[0, d].reshape(1, D_MODEL) for d in range(2)]
    zeros = jnp.zeros((bsz, 1, D_MODEL), F32)
    rc_f, _, _ = _rnn_in(ctx, csc1, csh1, w_in, conv_w, conv_b, wgate[0], bgate[0], ap[0],
                         zeros, tmc, False)
    rc_b, _, _ = _rnn_in(ctx, csc1, csh1, w_in, conv_w, conv_b, wgate[1], bgate[1], ap[1],
                         zeros, tmc, True)
    h0_f = rc_f[:, n_ctx - 1:n_ctx]
    h0_b = rc_b[:, 0:1]
    r_f, y, xb = _rnn_in(x, sc1, sh1, w_in, conv_w, conv_b, wgate[0], bgate[0], ap[0],
                         h0_f, tm, False)
    l1g, l1b = ln1_g[1].reshape(1, D_MODEL), ln1_b[1].reshape(1, D_MODEL)
    l2g, l2b = ln2_g[1].reshape(1, D_MODEL), ln2_b[1].reshape(1, D_MODEL)
    x = _rnn_out(xb, y, r_f, x, g1, wgate[1], bgate[1], ap[1], h0_b, w_out, l1g, l1b, tm)
    fw = _prep_ffn(ffn_w_up[1], ffn_conv_w[1], ffn_conv_b[1], ffn_w_down[1])
    x = _conv_ffn_ln(x, sc2, sh2, g2, fw, l2g, l2b, tm)
    return x
```

```python
import functools
import math

import jax
import jax.numpy as jnp
from jax import lax
from jax.experimental import pallas as pl
from jax.experimental.pallas import tpu as pltpu

F32 = jnp.float32
BF16 = jnp.bfloat16

D_MODEL = 1024
DEPTH = 2
GRID_W = 64
HD = 64
D_A = D_MODEL // 2
D_B = D_MODEL - D_A
H_A = D_A // (2 * HD)
H_B = D_B // HD
WIN_R = 8
WIN_C = 16
ROPE_THETA = 10000.0
N_RG_BLOCKS = 8
RG_BW = D_MODEL // N_RG_BLOCKS
RG_C = 8.0
CONV_RNN = 4
D_FF = ((8 * D_MODEL // 3 + 127) // 128) * 128
ALPHA = (2 * DEPTH) ** 0.25
NORM_EPS = 1e-5
NEG_BIG = -1e30
Q_SCALE = HD ** -0.5 * math.log2(math.e)
GELU_C1 = math.sqrt(2.0 / math.pi)
GELU_C2 = GELU_C1 * 0.044715

LANES = 128
SUBLANES = 8
HALO = 16
DENOM_ROWS = 16
FF_CHUNK = 256
N_FF_CHUNKS = D_FF // FF_CHUNK
VMEM_LIMIT = 56 * 1024 * 1024

_NT = (((1,), (1,)), ((), ()))


def _cparams(n_axes):
    return pltpu.CompilerParams(dimension_semantics=("arbitrary",) * n_axes,
                                vmem_limit_bytes=VMEM_LIMIT)


def _layer_norm(y, g, b):
    mu = jnp.mean(y, axis=-1, keepdims=True)
    yc = y - mu
    var = jnp.mean(yc * yc, axis=-1, keepdims=True)
    return yc * lax.rsqrt(var + NORM_EPS) * g + b


def _softmax_parts(s_list):
    m = s_list[0].max(axis=-1, keepdims=True)
    for s in s_list[1:]:
        m = jnp.maximum(m, s.max(axis=-1, keepdims=True))
    ps = [jnp.exp2(s - m) for s in s_list]
    l = ps[0].sum(axis=-1, keepdims=True)
    for p in ps[1:]:
        l = l + p.sum(axis=-1, keepdims=True)
    return ps, 1.0 / l


def _half_masks(dtype):
    lane = lax.broadcasted_iota(jnp.int32, (1, LANES), 1)
    lo = (lane < HD).astype(dtype)
    return lo, (1 - lo).astype(dtype)


def _ada_kernel(c_ref, w_ref, b_ref, o_ref):
    c = c_ref[...]
    a = c * jax.nn.sigmoid(c)
    o_ref[0] = jnp.dot(a, w_ref[0], preferred_element_type=F32,
                       precision=lax.Precision.HIGHEST) + b_ref[0]


def _ada_mod(cc, ada_w, ada_b):
    n = cc.shape[0]
    tn = 6 * D_MODEL // 4
    return pl.pallas_call(
        _ada_kernel,
        out_shape=jax.ShapeDtypeStruct((DEPTH, n, 6 * D_MODEL), F32),
        grid=(DEPTH, 6 * D_MODEL // tn),
        in_specs=[pl.BlockSpec((n, D_MODEL), lambda l, j: (0, 0)),
                  pl.BlockSpec((1, D_MODEL, tn), lambda l, j: (l, 0, j)),
                  pl.BlockSpec((1, 1, tn), lambda l, j: (l, 0, j))],
        out_specs=pl.BlockSpec((1, n, tn), lambda l, j: (l, 0, j)),
        compiler_params=_cparams(2), name="ada_mod",
    )(cc, ada_w, ada_b.reshape(DEPTH, 1, 6 * D_MODEL))


def _proj_attn_kernel(x_ref, sc_ref, sh_ref, w_ref, cos_ref, sin_ref,
                      qa_ref, ka_ref, va_ref, qb_ref, kb_ref, vb_ref, vt_ref, *, rope):
    h = (x_ref[0] * (1.0 + sc_ref[0]) + sh_ref[0]).astype(BF16)
    outs = (qa_ref, ka_ref, va_ref, qb_ref, kb_ref, vb_ref)
    for j, o_ref in enumerate(outs):
        r = jnp.dot(h, w_ref[:, j * D_A:(j + 1) * D_A], preferred_element_type=F32)
        if rope and j < 2:
            lane = lax.broadcasted_iota(jnp.int32, r.shape, 1)
            first_half = (lane % (HD // 2)) < (HD // 4)
            rot = jnp.where(first_half, pltpu.roll(r, D_A - HD // 4, 1), pltpu.roll(r, HD // 4, 1))
            r = r * cos_ref[...] + rot * sin_ref[...]
        if j in (0, 3):
            r = r * Q_SCALE
        if j in (2, 5):
            vt_ref[...] = r
            o_ref[0] = vt_ref[...].T.astype(BF16)
        else:
            o_ref[0] = r.astype(BF16)


def _proj_attn(x, sc, sh, w, cos_t, sin_t, rope, tm):
    b, t, _ = x.shape
    out = jax.ShapeDtypeStruct((b, t, D_A), BF16)
    out_t = jax.ShapeDtypeStruct((b, D_A, t), BF16)
    vec = pl.BlockSpec((1, 1, D_MODEL), lambda bi, i: (bi, 0, 0))
    blk = pl.BlockSpec((1, tm, D_A), lambda bi, i: (bi, i, 0))
    blk_t = pl.BlockSpec((1, D_A, tm), lambda bi, i: (bi, 0, i))
    tab = pl.BlockSpec((tm, D_A), lambda bi, i: (i, 0))
    return pl.pallas_call(
        functools.partial(_proj_attn_kernel, rope=rope),
        out_shape=(out, out, out_t, out, out, out_t),
        grid=(b, t // tm),
        in_specs=[pl.BlockSpec((1, tm, D_MODEL), lambda bi, i: (bi, i, 0)), vec, vec,
                  pl.BlockSpec((D_MODEL, 6 * D_A), lambda bi, i: (0, 0)), tab, tab],
        out_specs=(blk, blk, blk_t, blk, blk, blk_t),
        scratch_shapes=[pltpu.VMEM((tm, D_A), F32)],
        compiler_params=_cparams(2), name="proj_attn",
    )(x, sc, sh, w, cos_t, sin_t)


def _lambda_full(lp_ref, lambda_init):
    lp = lp_ref[...]
    s1 = jnp.sum(lp[0:1] * lp[1:2], axis=-1, keepdims=True)
    s2 = jnp.sum(lp[2:3] * lp[3:4], axis=-1, keepdims=True)
    return jnp.exp(s1) - jnp.exp(s2) + lambda_init


def _head_rms(o, g, lambda_init):
    o = o * lax.rsqrt(jnp.mean(o * o, axis=-1, keepdims=True) + NORM_EPS)
    return o * g * (1.0 - lambda_init)


def _diff_attn_kernel(lp_ref, g_ref, q_ref, kc_ref, vct_ref, kl_ref, vlt_ref, o_ref,
                      s_ref, p_ref, acc_ref, *, tk, n_chunks, n_ctx, lambda_init):
    lam = _lambda_full(lp_ref, lambda_init)
    q = q_ref[0]
    lo, hi = _half_masks(BF16)
    qs = (q * lo, q * hi)
    tq = q.shape[0]
    chunks = [(kc_ref, vct_ref, 0, n_ctx)] + [(kl_ref, vlt_ref, j * tk, tk) for j in range(n_chunks)]
    n_all = len(chunks)
    acc_ref[...] = jnp.zeros_like(acc_ref)

    def scores(c):
        k_ref, _, off, size = chunks[c]
        k = k_ref[0, off:off + size, :]
        for mi in range(2):
            s_ref[c % 2, mi, :size, :] = lax.dot_general(k, qs[mi], _NT, preferred_element_type=F32)

    def softmax(c, m_old):
        size = chunks[c][3]
        m_out, a_out = [], []
        for mi in range(2):
            s = s_ref[c % 2, mi, :size, :]
            m_new = jnp.maximum(m_old[mi], s.max(axis=0, keepdims=True))
            a_out.append(jnp.exp2(m_old[mi] - m_new))
            p_ref[c % 2, mi, :size, :] = jnp.exp2(s - m_new).astype(BF16)
            m_out.append(m_new)
        return m_out, a_out

    def values(c, a):
        _, vt_ref, off, size = chunks[c]
        vext = jnp.concatenate([vt_ref[0, :, off:off + size], jnp.ones((DENOM_ROWS, size), BF16)], axis=0)
        for mi in range(2):
            pv = jnp.dot(vext, p_ref[c % 2, mi, :size, :], preferred_element_type=F32)
            acc_ref[mi] = a[mi] * acc_ref[mi] + pv

    m = [jnp.full((1, tq), NEG_BIG, F32)] * 2
    a_next = None
    for i in range(-2, n_all):
        a_cur = a_next
        if i + 2 < n_all:
            scores(i + 2)
        if 0 <= i + 1 < n_all:
            m, a_next = softmax(i + 1, m)
        if i >= 0:
            values(i, a_cur)

    outs = []
    for mi in range(2):
        acc = acc_ref[mi]
        outs.append(acc[:2 * HD] / acc[2 * HD:2 * HD + 1])
    o = (outs[0] - lam * outs[1]).T
    o_ref[0] = _head_rms(o, g_ref[...], lambda_init).astype(BF16)


def _diff_attn(lp, g, qa, ka_c, vat_c, ka, vat, lambda_init, tq, tk):
    b, t, _ = qa.shape
    n_ctx = ka_c.shape[1]
    assert n_ctx <= tk and n_ctx % LANES == 0
    q_blk = pl.BlockSpec((1, tq, LANES), lambda bi, h, i: (bi, i, h))
    return pl.pallas_call(
        functools.partial(_diff_attn_kernel, tk=tk, n_chunks=t // tk, n_ctx=n_ctx, lambda_init=lambda_init),
        out_shape=jax.ShapeDtypeStruct((b, t, D_A), BF16),
        grid=(b, H_A, t // tq),
        in_specs=[pl.BlockSpec((4, HD), lambda bi, h, i: (0, 0)),
                  pl.BlockSpec((1, 2 * HD), lambda bi, h, i: (0, 0)),
                  q_blk,
                  pl.BlockSpec((1, n_ctx, LANES), lambda bi, h, i: (bi, 0, h)),
                  pl.BlockSpec((1, LANES, n_ctx), lambda bi, h, i: (bi, h, 0)),
                  pl.BlockSpec((1, t, LANES), lambda bi, h, i: (bi, 0, h)),
                  pl.BlockSpec((1, LANES, t), lambda bi, h, i: (bi, h, 0))],
        out_specs=q_blk,
        scratch_shapes=[pltpu.VMEM((2, 2, tk, tq), F32),
                        pltpu.VMEM((2, 2, tk, tq), BF16),
                        pltpu.VMEM((2, 2 * HD + DENOM_ROWS, tq), F32)],
        compiler_params=_cparams(3), name="diff_attn",
    )(lp, g, qa, ka_c, vat_c, ka, vat)


NA_QROWS = 4
NA_GROUPS = 4
NA_KROWS = NA_QROWS + WIN_R
NA_REL_PAD = NA_QROWS
NA_REL_N = NA_KROWS + WIN_R + NA_QROWS - 2


def _na_kernel(q_ref, k_ref, vt_ref, kc_ref, vct_ref, tab_ref, o_ref, s_ref, p_ref, ot_ref, *, rows, n_ctx):
    n_loc = NA_KROWS * GRID_W
    nq = NA_QROWS * GRID_W
    lo, hi = _half_masks(BF16)
    qr = lax.broadcasted_iota(jnp.int32, (1, nq), 1) // GRID_W

    def geometry(gi):
        r0 = (pl.program_id(1) * NA_GROUPS + gi) * NA_QROWS
        ws = jnp.clip(r0 - WIN_R // 2, 0, rows - NA_KROWS)
        rs = jnp.clip(r0 + qr - WIN_R // 2, 0, rows - WIN_R)
        row_mask = [jnp.where((ws + kr >= rs) & (ws + kr < rs + WIN_R), 0.0, NEG_BIG) for kr in range(NA_KROWS)]
        return r0 - ws, pl.multiple_of(ws * GRID_W, NA_QROWS * GRID_W), row_mask

    geo = [geometry(gi) for gi in range(NA_GROUPS)]
    items = [(gi, h) for gi in range(NA_GROUPS) for h in range(H_B)]

    def scores(it):
        gi, h = items[it]
        d0, k_start, row_mask = geo[gi]
        hp, e = divmod(h, 2)
        cols = slice(hp * LANES, (hp + 1) * LANES)
        qm = q_ref[0, gi * nq:(gi + 1) * nq, cols] * (lo, hi)[e]
        s_loc = lax.dot_general(k_ref[0, pl.ds(k_start, n_loc), cols], qm, _NT, preferred_element_type=F32)
        for kr in range(NA_KROWS):
            base = kr - d0 + (WIN_R - 1) + (NA_REL_PAD - 1)
            bias = jnp.concatenate([tab_ref[h, base - 2 * p] for p in range(NA_QROWS // 2)], axis=1)
            rows_kr = slice(kr * GRID_W, (kr + 1) * GRID_W)
            s_ref[it % 2, rows_kr, :] = s_loc[rows_kr] + bias + row_mask[kr]
        s_ref[it % 2, n_loc:, :] = lax.dot_general(kc_ref[0, :, cols], qm, _NT, preferred_element_type=F32)

    def softmax(it):
        s = s_ref[it % 2]
        m = s.max(axis=0, keepdims=True)
        p_ref[it % 2] = jnp.exp2(s - m).astype(BF16)

    def values(it):
        gi, h = items[it]
        k_start = geo[gi][1]
        hrows = slice(h * HD, (h + 1) * HD)
        vt = jnp.concatenate([vt_ref[0, hrows, pl.ds(k_start, n_loc)], vct_ref[0, hrows, :]], axis=1)
        vext = jnp.concatenate([vt, jnp.ones((DENOM_ROWS, n_loc + n_ctx), BF16)], axis=0)
        o = jnp.dot(vext, p_ref[it % 2], preferred_element_type=F32)
        ot_ref[gi, hrows, :] = o[:HD] / o[HD:HD + 1]

    n_items = len(items)
    for i in range(-2, n_items):
        if i + 2 < n_items:
            scores(i + 2)
        if 0 <= i + 1 < n_items:
            softmax(i + 1)
        if i >= 0:
            values(i)
    for gi in range(NA_GROUPS):
        o_ref[0, gi * nq:(gi + 1) * nq, :] = ot_ref[gi].T.astype(BF16)


def _na_bias_table(rpb):
    c = jnp.arange(GRID_W)[None, :]
    kc = jnp.arange(GRID_W)[:, None]
    cstart = jnp.clip(c - WIN_C // 2, 0, GRID_W - WIN_C)
    inside = (kc >= cstart) & (kc < cstart + WIN_C)
    rel_c = jnp.clip(kc - c + (WIN_C - 1), 0, 2 * WIN_C - 2)
    band = jnp.where(inside, rpb[:, :, rel_c] * math.log2(math.e), NEG_BIG)
    hi_pad = NA_REL_N - NA_REL_PAD + 1 - (2 * WIN_R - 1)
    band = jnp.pad(band, ((0, 0), (NA_REL_PAD, hi_pad), (0, 0), (0, 0)), constant_values=NEG_BIG)
    return jnp.concatenate([band[:, 1:], band[:, :-1]], axis=-1).astype(F32)


def _na_attn(qb, kb, vbt, kb_c, vbt_c, tab):
    b, t, _ = qb.shape
    rows = t // GRID_W
    n_ctx = kb_c.shape[1]
    step_rows = NA_QROWS * NA_GROUPS
    assert rows % step_rows == 0 and rows >= NA_KROWS and tab.shape[1] == NA_REL_N
    nq = NA_QROWS * GRID_W
    n_keys = NA_KROWS * GRID_W + n_ctx
    qblk = pl.BlockSpec((1, step_rows * GRID_W, D_B), lambda bi, g: (bi, g, 0))
    return pl.pallas_call(
        functools.partial(_na_kernel, rows=rows, n_ctx=n_ctx),
        out_shape=jax.ShapeDtypeStruct((b, t, D_B), BF16),
        grid=(b, rows // step_rows),
        in_specs=[qblk,
                  pl.BlockSpec((1, t, D_B), lambda bi, g: (bi, 0, 0)),
                  pl.BlockSpec((1, D_B, t), lambda bi, g: (bi, 0, 0)),
                  pl.BlockSpec((1, n_ctx, D_B), lambda bi, g: (bi, 0, 0)),
                  pl.BlockSpec((1, D_B, n_ctx), lambda bi, g: (bi, 0, 0)),
                  pl.BlockSpec(tab.shape, lambda bi, g: (0, 0, 0, 0))],
        out_specs=qblk,
        scratch_shapes=[pltpu.VMEM((2, n_keys, nq), F32),
                        pltpu.VMEM((2, n_keys, nq), BF16),
                        pltpu.VMEM((NA_GROUPS, D_B, nq), F32)],
        compiler_params=_cparams(2), name="na_attn",
    )(qb, kb, vbt, kb_c, vbt_c, tab)


def _ctx_attn_kernel(lp_ref, g_ref, qa_ref, ka_ref, vat_ref, qb_ref, kb_ref, vbt_ref,
                     oa_ref, ob_ref, *, lambda_init):
    lam = _lambda_full(lp_ref, lambda_init)
    lo, hi = _half_masks(BF16)
    lane = lax.broadcasted_iota(jnp.int32, (1, LANES), 1)
    for h in range(H_A):
        cols = slice(h * LANES, (h + 1) * LANES)
        q, k, vt = qa_ref[0, :, cols], ka_ref[0, :, cols], vat_ref[0, cols, :]
        probs = []
        for msk in (lo, hi):
            s = lax.dot_general(q * msk, k, _NT, preferred_element_type=F32)
            (p,), inv = _softmax_parts([s])
            probs.append(p * inv)
        w = (probs[0] - lam * probs[1]).astype(BF16)
        o = lax.dot_general(w, vt, _NT, preferred_element_type=F32)
        oa_ref[0, :, cols] = _head_rms(o, g_ref[...], lambda_init).astype(BF16)
    for hp in range(H_B // 2):
        cols = slice(hp * LANES, (hp + 1) * LANES)
        q, k, vt = qb_ref[0, :, cols], kb_ref[0, :, cols], vbt_ref[0, cols, :]
        halves = []
        for msk in (lo, hi):
            s = lax.dot_general(q * msk, k, _NT, preferred_element_type=F32)
            (p,), inv = _softmax_parts([s])
            halves.append(lax.dot_general((p * inv).astype(BF16), vt, _NT, preferred_element_type=F32))
        ob_ref[0, :, cols] = jnp.where(lane < HD, halves[0], halves[1]).astype(BF16)


def _ctx_attn(lp, g, qa, ka, vat, qb, kb, vbt, lambda_init):
    b, n, _ = qa.shape
    blk = pl.BlockSpec((1, n, D_A), lambda bi: (bi, 0, 0))
    blk_t = pl.BlockSpec((1, D_A, n), lambda bi: (bi, 0, 0))
    out = jax.ShapeDtypeStruct((b, n, D_A), BF16)
    return pl.pallas_call(
        functools.partial(_ctx_attn_kernel, lambda_init=lambda_init),
        out_shape=(out, out),
        grid=(b,),
        in_specs=[pl.BlockSpec((4, HD), lambda bi: (0, 0)),
                  pl.BlockSpec((1, 2 * HD), lambda bi: (0, 0)),
                  blk, blk, blk_t, blk, blk, blk_t],
        out_specs=(blk, blk),
        compiler_params=_cparams(1), name="ctx_attn",
    )(lp, g, qa, ka, vat, qb, kb, vbt)


def _outproj_ln_kernel(oa_ref, ob_ref, w_ref, x_ref, gate_ref, g_ref, b_ref, o_ref):
    o = (jnp.dot(oa_ref[0], w_ref[:D_A], preferred_element_type=F32)
         + jnp.dot(ob_ref[0], w_ref[D_A:], preferred_element_type=F32))
    y = ALPHA * x_ref[0] + gate_ref[0] * o
    o_ref[0] = _layer_norm(y, g_ref[...], b_ref[...])


def _outproj_ln(oa, ob, w, x, gate, ln_g, ln_b, tm):
    b, t, _ = x.shape
    half = pl.BlockSpec((1, tm, D_A), lambda bi, i: (bi, i, 0))
    full = pl.BlockSpec((1, tm, D_MODEL), lambda bi, i: (bi, i, 0))
    vec = pl.BlockSpec((1, 1, D_MODEL), lambda bi, i: (bi, 0, 0))
    par = pl.BlockSpec((1, D_MODEL), lambda bi, i: (0, 0))
    return pl.pallas_call(
        _outproj_ln_kernel,
        out_shape=jax.ShapeDtypeStruct((b, t, D_MODEL), F32),
        grid=(b, t // tm),
        in_specs=[half, half, pl.BlockSpec((D_MODEL, D_MODEL), lambda bi, i: (0, 0)),
                  full, vec, par, par],
        out_specs=full,
        compiler_params=_cparams(2), name="outproj_ln",
    )(oa, ob, w, x, gate, ln_g, ln_b)


def _fill_modulated(h_ref, x_ref, xp_ref, xn_ref, sc_ref, sh_ref, tm, reverse=False):
    i = pl.program_id(1)
    last = pl.num_programs(1) - 1
    ti = (last - i) if reverse else i
    scale = 1.0 + sc_ref[0]
    shift = sh_ref[0]
    h_ref[HALO:HALO + tm] = (x_ref[0] * scale + shift).astype(BF16)
    prev = xp_ref[0] * scale + shift
    nxt = xn_ref[0] * scale + shift
    h_ref[0:HALO] = jnp.where(ti > 0, prev, 0.0).astype(BF16)
    h_ref[HALO + tm:] = jnp.where(ti < last, nxt, 0.0).astype(BF16)


def _conv_ffn_kernel(x_ref, xp_ref, xn_ref, sc_ref, sh_ref, gate_ref, wu_ref, cw_ref, cb_ref, wd_ref,
                     lg_ref, lb_ref, o_ref, h_ref, ug_ref, uv_ref, a_ref, *, tm):
    _fill_modulated(h_ref, x_ref, xp_ref, xn_ref, sc_ref, sh_ref, tm)

    def conv3(u_ref, slot, wcols):
        u = u_ref[slot]
        cw = cw_ref[:, :, wcols]
        cb = cb_ref[:, wcols]
        rows = u.shape[0]
        u3 = u.reshape(rows // SUBLANES, SUBLANES, FF_CHUNK)
        sub = lax.broadcasted_iota(jnp.int32, u3.shape, 1)
        dn = pltpu.roll(u3, 1, 1)
        dn = jnp.where(sub == 0, jnp.concatenate([dn[-1:], dn[:-1]], axis=0), dn)
        up_ = pltpu.roll(u3, SUBLANES - 1, 1)
        up_ = jnp.where(sub == SUBLANES - 1, jnp.concatenate([up_[1:], up_[:1]], axis=0), up_)
        full = cw[0][None] * dn + cw[1][None] * u3 + cw[2][None] * up_ + cb[None]
        return full.reshape(rows, FF_CHUNK)[HALO:HALO + tm]

    def gate_cols(c):
        return slice(c * FF_CHUNK, (c + 1) * FF_CHUNK)

    def value_cols(c):
        return slice(D_FF + c * FF_CHUNK, D_FF + (c + 1) * FF_CHUNK)

    def up(c):
        h = h_ref[...]
        ug_ref[c % 2] = jnp.dot(h, wu_ref[:, gate_cols(c)], preferred_element_type=F32)
        uv_ref[c % 2] = jnp.dot(h, wu_ref[:, value_cols(c)], preferred_element_type=F32)

    def act(c):
        g = conv3(ug_ref, c % 2, gate_cols(c))
        v = conv3(uv_ref, c % 2, value_cols(c))
        t = jnp.tanh(g * (GELU_C1 + GELU_C2 * (g * g)))
        a_ref[:, gate_cols(c)] = ((g + g * t) * v).astype(BF16)

    up(0)
    for c in range(1, N_FF_CHUNKS):
        up(c)
        act(c - 1)
    act(N_FF_CHUNKS - 1)
    o = jnp.dot(a_ref[...], wd_ref[...], preferred_element_type=F32)
    y = ALPHA * x_ref[0] + gate_ref[0] * o
    o_ref[0] = _layer_norm(y, lg_ref[...], lb_ref[...])


def _halo_specs(t, tm):
    nb = t // HALO
    per = tm // HALO
    main = pl.BlockSpec((1, tm, D_MODEL), lambda bi, i: (bi, i, 0))
    prev = pl.BlockSpec((1, HALO, D_MODEL), lambda bi, i: (bi, jnp.maximum(i * per - 1, 0), 0))
    nxt = pl.BlockSpec((1, HALO, D_MODEL), lambda bi, i: (bi, jnp.minimum((i + 1) * per, nb - 1), 0))
    return main, prev, nxt


def _const_spec(shape):
    nd = len(shape)
    return pl.BlockSpec(shape, lambda bi, i: (0,) * nd, pipeline_mode=pl.Buffered(1))


def _conv_ffn_ln(x, sc, sh, gate, fw, ln_g, ln_b, tm):
    b, t, _ = x.shape
    main, prev, nxt = _halo_specs(t, tm)
    vec = pl.BlockSpec((1, 1, D_MODEL), lambda bi, i: (bi, 0, 0))
    par = _const_spec((1, D_MODEL))
    wu, cw, cb, wd = fw
    return pl.pallas_call(
        functools.partial(_conv_ffn_kernel, tm=tm),
        out_shape=jax.ShapeDtypeStruct((b, t, D_MODEL), F32),
        grid=(b, t // tm),
        in_specs=[main, prev, nxt, vec, vec, vec,
                  _const_spec(wu.shape), _const_spec(cw.shape), _const_spec(cb.shape),
                  _const_spec(wd.shape), par, par],
        out_specs=main,
        scratch_shapes=[pltpu.VMEM((tm + 2 * HALO, D_MODEL), BF16),
                        pltpu.VMEM((2, tm + 2 * HALO, FF_CHUNK), F32),
                        pltpu.VMEM((2, tm + 2 * HALO, FF_CHUNK), F32),
                        pltpu.VMEM((tm, D_FF), BF16)],
        compiler_params=_cparams(2), name="conv_ffn_ln",
    )(x, x, x, sc, sh, gate, wu, cw, cb, wd, ln_g, ln_b)


def _prep_ffn(w_up, conv_w, conv_b, w_down):
    half_v = jnp.concatenate([jnp.ones((D_FF,), F32), jnp.full((D_FF,), 0.5, F32)])
    cw = jnp.broadcast_to((conv_w * half_v)[:, None, :], (conv_w.shape[0], SUBLANES, 2 * D_FF))
    cb = jnp.broadcast_to((conv_b * half_v)[None, :], (SUBLANES, 2 * D_FF))
    return w_up.astype(BF16), cw, cb, w_down.astype(BF16)


def _rg_coeffs(xb, wgate_ref, bgate_ref, ap_ref, a_ref, b_ref, reverse):
    ap = ap_ref[...]
    neg = -ap
    sp = jnp.maximum(neg, 0.0) + jnp.log1p(jnp.exp(-jnp.abs(neg)))
    half_c = (-0.5 * RG_C) * sp
    for n in range(N_RG_BLOCKS):
        cols = slice(n * RG_BW, (n + 1) * RG_BW)
        xn = xb[:, cols]
        t = jnp.tanh(jnp.dot(xn.astype(BF16), wgate_ref[n], preferred_element_type=F32) + bgate_ref[n])
        log_a = half_c[:, cols] + half_c[:, cols] * t[:, :RG_BW]
        gate_x = 0.5 + 0.5 * t[:, RG_BW:]
        a = jnp.exp(log_a)
        one_minus_a2 = jnp.tanh(log_a) * (-1.0 - a * a)
        root = jnp.where(one_minus_a2 > 0.0, one_minus_a2 * lax.rsqrt(one_minus_a2), 0.0)
        a_ref[:, cols], b_ref[:, cols] = _group_prefix(a, gate_x * xn * root, reverse)


def _group_prefix(a, b, reverse):
    rows, cols = a.shape
    a = a.reshape(rows // SUBLANES, SUBLANES, cols)
    b = b.reshape(rows // SUBLANES, SUBLANES, cols)
    sub = lax.broadcasted_iota(jnp.int32, a.shape, 1)
    for k in (1, 2, 4):
        shift = (SUBLANES - k) if reverse else k
        msk = (sub < SUBLANES - k) if reverse else (sub >= k)
        a_sh = pltpu.roll(a, shift, 1)
        b_sh = pltpu.roll(b, shift, 1)
        b = jnp.where(msk, a * b_sh + b, b)
        a = jnp.where(msk, a * a_sh, a)
    return a.reshape(rows, cols), b.reshape(rows, cols)


def _scan_tile(a_ref, b_ref, r_ref, h_in, tt, reverse):
    n_groups = tt // SUBLANES
    h = h_in
    for gi in range(n_groups):
        g = (n_groups - 1 - gi) if reverse else gi
        rows = slice(g * SUBLANES, (g + 1) * SUBLANES)
        hg = a_ref[rows, :] * h + b_ref[rows, :]
        r_ref[rows, :] = hg
        h = hg[0:1] if reverse else hg[SUBLANES - 1:SUBLANES]
    return h


def _rnn_in_kernel(x_ref, xp_ref, xn_ref, sc_ref, sh_ref, w_ref, cw_ref, cb_ref,
                   wgate_ref, bgate_ref, ap_ref, h0_ref,
                   r_ref, y_ref, xb_ref, h_ref, a_ref, b_ref, carry_ref, *, tt, reverse):
    i = pl.program_id(1)

    @pl.when(i == 0)
    def _():
        carry_ref[...] = h0_ref[0]

    _fill_modulated(h_ref, x_ref, xp_ref, xn_ref, sc_ref, sh_ref, tt, reverse)
    h = h_ref[...]
    y_ref[0] = jnp.dot(h[HALO:HALO + tt], w_ref[:, :D_MODEL], preferred_element_type=F32).astype(BF16)
    u = jnp.dot(h, w_ref[:, D_MODEL:], preferred_element_type=F32)
    u3 = u.reshape((tt + 2 * HALO) // SUBLANES, SUBLANES, D_MODEL)
    sub = lax.broadcasted_iota(jnp.int32, u3.shape, 1)

    def shifted(k):
        rolled = pltpu.roll(u3, k % SUBLANES, 1)
        if k > 0:
            return jnp.where(sub < k, jnp.concatenate([rolled[-1:], rolled[:-1]], axis=0), rolled)
        return jnp.where(sub >= SUBLANES + k, jnp.concatenate([rolled[1:], rolled[:1]], axis=0), rolled)

    full = (cw_ref[0][None] * shifted(2) + cw_ref[1][None] * shifted(1) + cw_ref[2][None] * u3
            + cw_ref[3][None] * shifted(-1) + cb_ref[...][None])
    xb = full.reshape(tt + 2 * HALO, D_MODEL)[HALO:HALO + tt]
    xb_ref[0] = xb
    _rg_coeffs(xb, wgate_ref, bgate_ref, ap_ref, a_ref, b_ref, reverse)
    carry_ref[...] = _scan_tile(a_ref, b_ref, r_ref.at[0], carry_ref[...], tt, reverse)


def _rnn_in(x, sc, sh, w_in, conv_w, conv_b, wgate, bgate, ap, h0, tt, reverse):
    b, t, _ = x.shape
    nt = t // tt
    nb = t // HALO
    per = tt // HALO

    def tix(i):
        return (nt - 1 - i) if reverse else i

    main = pl.BlockSpec((1, tt, D_MODEL), lambda bi, i: (bi, tix(i), 0))
    prev = pl.BlockSpec((1, HALO, D_MODEL), lambda bi, i: (bi, jnp.maximum(tix(i) * per - 1, 0), 0))
    nxt = pl.BlockSpec((1, HALO, D_MODEL), lambda bi, i: (bi, jnp.minimum((tix(i) + 1) * per, nb - 1), 0))
    vec = pl.BlockSpec((1, 1, D_MODEL), lambda bi, i: (bi, 0, 0))
    return pl.pallas_call(
        functools.partial(_rnn_in_kernel, tt=tt, reverse=reverse),
        out_shape=(jax.ShapeDtypeStruct((b, t, D_MODEL), F32),
                   jax.ShapeDtypeStruct((b, t, D_MODEL), BF16),
                   jax.ShapeDtypeStruct((b, t, D_MODEL), F32)),
        grid=(b, nt),
        in_specs=[main, prev, nxt, vec, vec,
                  _const_spec(w_in.shape), _const_spec(conv_w.shape), _const_spec(conv_b.shape),
                  _const_spec(wgate.shape), _const_spec(bgate.shape), _const_spec((1, D_MODEL)), vec],
        out_specs=(main, main, main),
        scratch_shapes=[pltpu.VMEM((tt + 2 * HALO, D_MODEL), BF16),
                        pltpu.VMEM((tt, D_MODEL), F32),
                        pltpu.VMEM((tt, D_MODEL), F32),
                        pltpu.VMEM((1, D_MODEL), F32)],
        compiler_params=_cparams(2), name="rnn_in_rev" if reverse else "rnn_in_fwd",
    )(x, x, x, sc, sh, w_in, conv_w, conv_b, wgate, bgate, ap, h0)


def _rnn_out_kernel(xb_ref, y_ref, rf_ref, x_ref, gate_ref, wgate_ref, bgate_ref, ap_ref, h0_ref,
                    w_ref, lg_ref, lb_ref, o_ref, a_ref, b_ref, rb_ref, carry_ref, *, tt):
    i = pl.program_id(1)

    @pl.when(i == 0)
    def _():
        carry_ref[...] = h0_ref[0]

    _rg_coeffs(xb_ref[0], wgate_ref, bgate_ref, ap_ref, a_ref, b_ref, True)
    carry_ref[...] = _scan_tile(a_ref, b_ref, rb_ref, carry_ref[...], tt, True)
    r = rf_ref[0] + rb_ref[...]
    yg = y_ref[0].astype(F32)
    hy = 0.5 * yg
    m = (r * (hy + hy * jnp.tanh(yg * (GELU_C1 + GELU_C2 * (yg * yg))))).astype(BF16)
    o = jnp.dot(m, w_ref[...], preferred_element_type=F32)
    y = ALPHA * x_ref[0] + gate_ref[0] * o
    o_ref[0] = _layer_norm(y, lg_ref[...], lb_ref[...])


def _rnn_out(xb, y, rf, x, gate, wgate, bgate, ap, h0, w_out, ln_g, ln_b, tt):
    b, t, _ = x.shape
    nt = t // tt
    main = pl.BlockSpec((1, tt, D_MODEL), lambda bi, i: (bi, nt - 1 - i, 0))
    vec = pl.BlockSpec((1, 1, D_MODEL), lambda bi, i: (bi, 0, 0))
    par = _const_spec((1, D_MODEL))
    return pl.pallas_call(
        functools.partial(_rnn_out_kernel, tt=tt),
        out_shape=jax.ShapeDtypeStruct((b, t, D_MODEL), F32),
        grid=(b, nt),
        in_specs=[main, main, main, main, vec,
                  _const_spec(wgate.shape), _const_spec(bgate.shape), par, vec,
                  _const_spec(w_out.shape), par, par],
        out_specs=main,
        scratch_shapes=[pltpu.VMEM((tt, D_MODEL), F32)] * 3 + [pltpu.VMEM((1, D_MODEL), F32)],
        compiler_params=_cparams(2), name="rnn_out",
    )(xb, y, rf, x, gate, wgate, bgate, ap, h0, w_out, ln_g, ln_b)


def _rope_tables(n_tok):
    t = jnp.arange(n_tok)
    row = (t // GRID_W).astype(F32)[:, None]
    col = (t % GRID_W).astype(F32)[:, None]
    nf = HD // 4
    inv = 1.0 / (ROPE_THETA ** (jnp.arange(nf, dtype=F32) / nf))
    ang = jnp.concatenate([row * inv, row * inv, col * inv, col * inv], -1)
    sign = jnp.where((jnp.arange(HD) % (HD // 2)) < nf, -1.0, 1.0)
    reps = D_A // HD
    return jnp.tile(jnp.cos(ang), (1, reps)), jnp.tile(jnp.sin(ang) * sign, (1, reps))


ROW_TILE = 512
ATTN_Q_TILE = 1024
ATTN_K_CHUNK = 512


def _tile(n, pref):
    return pref if n % pref == 0 else n


def kernel(x, c, ctx, c_ctx, ada_w, ada_b, ln1_g, ln1_b, ln2_g, ln2_b, ffn_w_up, ffn_conv_w, ffn_conv_b, ffn_w_down, att_w_in, att_w_out, diff_lq1, diff_lk1, diff_lq2, diff_lk2, diff_subln_g, na_rpb, rnn_w_in, rnn_conv_w, rnn_conv_b, rg_a_param, rg_wa, rg_ba, rg_wx, rg_bx, rnn_w_out):
    bsz, t, _ = x.shape
    n_ctx = ctx.shape[1]
    assert t % GRID_W == 0 and t // GRID_W >= WIN_R and n_ctx % HALO == 0

    n_rows = -(-(bsz + 1) // SUBLANES) * SUBLANES
    cc = jnp.zeros((n_rows, D_MODEL), F32).at[:bsz].set(c).at[bsz].set(c_ctx)
    mod = _ada_mod(cc, ada_w, ada_b)

    def mods(l):
        lat = mod[l, :bsz].reshape(bsz, 1, 6, D_MODEL)
        cx = jnp.broadcast_to(mod[l, bsz].reshape(1, 1, 6, D_MODEL), (bsz, 1, 6, D_MODEL))
        return ([lat[:, :, k] for k in range(6)], [cx[:, :, k] for k in range(6)])

    tm = _tile(t, ROW_TILE)
    tmc = _tile(n_ctx, ROW_TILE)

    (sh1, sc1, g1, sh2, sc2, g2), (csh1, csc1, cg1, csh2, csc2, cg2) = mods(0)
    lambda_init = 0.8 - 0.6 * math.exp(-0.3 * 0)
    w_in = att_w_in[0].astype(BF16)
    w_out = att_w_out[0].astype(BF16)
    cos_t, sin_t = _rope_tables(t)
    qa, ka, vat, qb, kb, vbt = _proj_attn(x, sc1, sh1, w_in, cos_t, sin_t, True, tm)
    qa_c, ka_c, vat_c, qb_c, kb_c, vbt_c = _proj_attn(
        ctx, csc1, csh1, w_in, cos_t[:n_ctx], sin_t[:n_ctx], False, tmc)
    lp = jnp.stack([diff_lq1[0], diff_lk1[0], diff_lq2[0], diff_lk2[0]])
    sub_g = diff_subln_g[0].reshape(1, 2 * HD)
    oa = _diff_attn(lp, sub_g, qa, ka_c, vat_c, ka, vat, lambda_init,
                    _tile(t, ATTN_Q_TILE), _tile(t, ATTN_K_CHUNK))
    ob = _na_attn(qb, kb, vbt, kb_c, vbt_c, _na_bias_table(na_rpb[0]))
    oa_c, ob_c = _ctx_attn(lp, sub_g, qa_c, ka_c, vat_c, qb_c, kb_c, vbt_c, lambda_init)
    l1g, l1b = ln1_g[0].reshape(1, D_MODEL), ln1_b[0].reshape(1, D_MODEL)
    l2g, l2b = ln2_g[0].reshape(1, D_MODEL), ln2_b[0].reshape(1, D_MODEL)
    x = _outproj_ln(oa, ob, w_out, x, g1, l1g, l1b, tm)
    ctx = _outproj_ln(oa_c, ob_c, w_out, ctx, cg1, l1g, l1b, tmc)
    fw = _prep_ffn(ffn_w_up[0], ffn_conv_w[0], ffn_conv_b[0], ffn_w_down[0])
    x = _conv_ffn_ln(x, sc2, sh2, g2, fw, l2g, l2b, tm)
    ctx = _conv_ffn_ln(ctx, csc2, csh2, cg2, fw, l2g, l2b, tmc)

    (sh1, sc1, g1, sh2, sc2, g2), (csh1, csc1, _, _, _, _) = mods(1)
    w_in = rnn_w_in[0].astype(BF16)
    w_out = rnn_w_out[0].astype(BF16)
    conv_w = jnp.broadcast_to(rnn_conv_w[0][:, None, :], (CONV_RNN, SUBLANES, D_MODEL))
    conv_b = jnp.broadcast_to(rnn_conv_b[0][None, :], (SUBLANES, D_MODEL))
    wgate = [(0.5 * jnp.concatenate([rg_wa[0, d], rg_wx[0, d]], axis=-1)).astype(BF16) for d in range(2)]
    bgate = [0.5 * jnp.concatenate([rg_ba[0, d], rg_bx[0, d]], axis=-1).reshape(N_RG_BLOCKS, 1, 2 * RG_BW)
             for d in range(2)]
    ap = [rg_a_param[0, d].reshape(1, D_MODEL) for d in range(2)]
    zeros = jnp.zeros((bsz, 1, D_MODEL), F32)
    rc_f, _, _ = _rnn_in(ctx, csc1, csh1, w_in, conv_w, conv_b, wgate[0], bgate[0], ap[0],
                         zeros, tmc, False)
    rc_b, _, _ = _rnn_in(ctx, csc1, csh1, w_in, conv_w, conv_b, wgate[1], bgate[1], ap[1],
                         zeros, tmc, True)
    h0_f = rc_f[:, n_ctx - 1:n_ctx]
    h0_b = rc_b[:, 0:1]
    r_f, y, xb = _rnn_in(x, sc1, sh1, w_in, conv_w, conv_b, wgate[0], bgate[0], ap[0],
                         h0_f, tm, False)
    l1g, l1b = ln1_g[1].reshape(1, D_MODEL), ln1_b[1].reshape(1, D_MODEL)
    l2g, l2b = ln2_g[1].reshape(1, D_MODEL), ln2_b[1].reshape(1, D_MODEL)
    x = _rnn_out(xb, y, r_f, x, g1, wgate[1], bgate[1], ap[1], h0_b, w_out, l1g, l1b, tm)
    fw = _prep_ffn(ffn_w_up[1], ffn_conv_w[1], ffn_conv_b[1], ffn_w_down[1])
    x = _conv_ffn_ln(x, sc2, sh2, g2, fw, l2g, l2b, tm)
    return x
```

```python
import functools
import math

import jax
import jax.numpy as jnp
from jax import lax
from jax.experimental import pallas as pl
from jax.experimental.pallas import tpu as pltpu

F32 = jnp.float32
BF16 = jnp.bfloat16

D_MODEL = 1024
DEPTH = 2
GRID_W = 64
HD = 64
D_A = D_MODEL // 2
D_B = D_MODEL - D_A
H_A = D_A // (2 * HD)
H_B = D_B // HD
WIN_R = 8
WIN_C = 16
ROPE_THETA = 10000.0
N_RG_BLOCKS = 8
RG_BW = D_MODEL // N_RG_BLOCKS
RG_C = 8.0
CONV_RNN = 4
D_FF = ((8 * D_MODEL // 3 + 127) // 128) * 128
ALPHA = (2 * DEPTH) ** 0.25
NORM_EPS = 1e-5
NEG_BIG = -1e30
Q_SCALE = HD ** -0.5 * math.log2(math.e)
GELU_C1 = math.sqrt(2.0 / math.pi)
GELU_C2 = GELU_C1 * 0.044715

LANES = 128
SUBLANES = 8
HALO = 16
DENOM_ROWS = 16
FF_CHUNK = 256
N_FF_CHUNKS = D_FF // FF_CHUNK
VMEM_LIMIT = 56 * 1024 * 1024

_NT = (((1,), (1,)), ((), ()))


def _cparams(n_axes):
    return pltpu.CompilerParams(dimension_semantics=("arbitrary",) * n_axes,
                                vmem_limit_bytes=VMEM_LIMIT)


def _layer_norm(y, g, b):
    mu = jnp.mean(y, axis=-1, keepdims=True)
    yc = y - mu
    var = jnp.mean(yc * yc, axis=-1, keepdims=True)
    return yc * lax.rsqrt(var + NORM_EPS) * g + b


def _softmax_parts(s_list):
    m = s_list[0].max(axis=-1, keepdims=True)
    for s in s_list[1:]:
        m = jnp.maximum(m, s.max(axis=-1, keepdims=True))
    ps = [jnp.exp2(s - m) for s in s_list]
    l = ps[0].sum(axis=-1, keepdims=True)
    for p in ps[1:]:
        l = l + p.sum(axis=-1, keepdims=True)
    return ps, 1.0 / l


def _half_masks(dtype):
    lane = lax.broadcasted_iota(jnp.int32, (1, LANES), 1)
    lo = (lane < HD).astype(dtype)
    return lo, (1 - lo).astype(dtype)


def _ada_kernel(c_ref, w_ref, b_ref, o_ref):
    c = c_ref[...]
    a = c * jax.nn.sigmoid(c)
    o_ref[0] = jnp.dot(a, w_ref[0], preferred_element_type=F32,
                       precision=lax.Precision.HIGHEST) + b_ref[0]


def _ada_mod(cc, ada_w, ada_b):
    n = cc.shape[0]
    tn = 6 * D_MODEL // 4
    return pl.pallas_call(
        _ada_kernel,
        out_shape=jax.ShapeDtypeStruct((DEPTH, n, 6 * D_MODEL), F32),
        grid=(DEPTH, 6 * D_MODEL // tn),
        in_specs=[pl.BlockSpec((n, D_MODEL), lambda l, j: (0, 0)),
                  pl.BlockSpec((1, D_MODEL, tn), lambda l, j: (l, 0, j)),
                  pl.BlockSpec((1, 1, tn), lambda l, j: (l, 0, j))],
        out_specs=pl.BlockSpec((1, n, tn), lambda l, j: (l, 0, j)),
        compiler_params=_cparams(2), name="ada_mod",
    )(cc, ada_w, ada_b.reshape(DEPTH, 1, 6 * D_MODEL))


def _proj_attn_kernel(x_ref, sc_ref, sh_ref, w_ref, cos_ref, sin_ref,
                      qa_ref, ka_ref, va_ref, qb_ref, kb_ref, vb_ref, vt_ref, *, rope):
    h = (x_ref[0] * (1.0 + sc_ref[0]) + sh_ref[0]).astype(BF16)
    outs = (qa_ref, ka_ref, va_ref, qb_ref, kb_ref, vb_ref)
    for j, o_ref in enumerate(outs):
        r = jnp.dot(h, w_ref[:, j * D_A:(j + 1) * D_A], preferred_element_type=F32)
        if rope and j < 2:
            lane = lax.broadcasted_iota(jnp.int32, r.shape, 1)
            first_half = (lane % (HD // 2)) < (HD // 4)
            rot = jnp.where(first_half, pltpu.roll(r, D_A - HD // 4, 1), pltpu.roll(r, HD // 4, 1))
            r = r * cos_ref[...] + rot * sin_ref[...]
        if j in (0, 3):
            r = r * Q_SCALE
        if j in (2, 5):
            vt_ref[...] = r
            o_ref[0] = vt_ref[...].T.astype(BF16)
        else:
            o_ref[0] = r.astype(BF16)


def _proj_attn(x, sc, sh, w, cos_t, sin_t, rope, tm):
    b, t, _ = x.shape
    out = jax.ShapeDtypeStruct((b, t, D_A), BF16)
    out_t = jax.ShapeDtypeStruct((b, D_A, t), BF16)
    vec = pl.BlockSpec((1, 1, D_MODEL), lambda bi, i: (bi, 0, 0))
    blk = pl.BlockSpec((1, tm, D_A), lambda bi, i: (bi, i, 0))
    blk_t = pl.BlockSpec((1, D_A, tm), lambda bi, i: (bi, 0, i))
    tab = pl.BlockSpec((tm, D_A), lambda bi, i: (i, 0))
    return pl.pallas_call(
        functools.partial(_proj_attn_kernel, rope=rope),
        out_shape=(out, out, out_t, out, out, out_t),
        grid=(b, t // tm),
        in_specs=[pl.BlockSpec((1, tm, D_MODEL), lambda bi, i: (bi, i, 0)), vec, vec,
                  pl.BlockSpec((D_MODEL, 6 * D_A), lambda bi, i: (0, 0)), tab, tab],
        out_specs=(blk, blk, blk_t, blk, blk, blk_t),
        scratch_shapes=[pltpu.VMEM((tm, D_A), F32)],
        compiler_params=_cparams(2), name="proj_attn",
    )(x, sc, sh, w, cos_t, sin_t)


def _lambda_full(lp_ref, lambda_init):
    lp = lp_ref[...]
    s1 = jnp.sum(lp[0:1] * lp[1:2], axis=-1, keepdims=True)
    s2 = jnp.sum(lp[2:3] * lp[3:4], axis=-1, keepdims=True)
    return jnp.exp(s1) - jnp.exp(s2) + lambda_init


def _head_rms(o, g, lambda_init):
    o = o * lax.rsqrt(jnp.mean(o * o, axis=-1, keepdims=True) + NORM_EPS)
    return o * g * (1.0 - lambda_init)


def _diff_attn_kernel(lp_ref, g_ref, q_ref, kc_ref, vct_ref, kl_ref, vlt_ref, o_ref,
                      s_ref, p_ref, acc_ref, *, tk, n_chunks, n_ctx, lambda_init):
    lam = _lambda_full(lp_ref, lambda_init)
    q = q_ref[0]
    lo, hi = _half_masks(BF16)
    qs = (q * lo, q * hi)
    tq = q.shape[0]
    chunks = [(kc_ref, vct_ref, 0, n_ctx)] + [(kl_ref, vlt_ref, j * tk, tk) for j in range(n_chunks)]
    n_all = len(chunks)
    acc_ref[...] = jnp.zeros_like(acc_ref)

    def scores(c):
        k_ref, _, off, size = chunks[c]
        k = k_ref[0, off:off + size, :]
        for mi in range(2):
            s_ref[c % 2, mi, :size, :] = lax.dot_general(k, qs[mi], _NT, preferred_element_type=F32)

    def softmax(c, m_old):
        size = chunks[c][3]
        m_out, a_out = [], []
        for mi in range(2):
            s = s_ref[c % 2, mi, :size, :]
            m_new = jnp.maximum(m_old[mi], s.max(axis=0, keepdims=True))
            a_out.append(jnp.exp2(m_old[mi] - m_new))
            p_ref[c % 2, mi, :size, :] = jnp.exp2(s - m_new).astype(BF16)
            m_out.append(m_new)
        return m_out, a_out

    def values(c, a):
        _, vt_ref, off, size = chunks[c]
        vext = jnp.concatenate([vt_ref[0, :, off:off + size], jnp.ones((DENOM_ROWS, size), BF16)], axis=0)
        for mi in range(2):
            pv = jnp.dot(vext, p_ref[c % 2, mi, :size, :], preferred_element_type=F32)
            acc_ref[mi] = a[mi] * acc_ref[mi] + pv

    m = [jnp.full((1, tq), NEG_BIG, F32)] * 2
    a_next = None
    for i in range(-2, n_all):
        a_cur = a_next
        if i + 2 < n_all:
            scores(i + 2)
        if 0 <= i + 1 < n_all:
            m, a_next = softmax(i + 1, m)
        if i >= 0:
            values(i, a_cur)

    outs = []
    for mi in range(2):
        acc = acc_ref[mi]
        outs.append(acc[:2 * HD] / acc[2 * HD:2 * HD + 1])
    o = (outs[0] - lam * outs[1]).T
    o_ref[0] = _head_rms(o, g_ref[...], lambda_init).astype(BF16)


def _diff_attn(lp, g, qa, ka_c, vat_c, ka, vat, lambda_init, tq, tk):
    b, t, _ = qa.shape
    n_ctx = ka_c.shape[1]
    assert n_ctx <= tk and n_ctx % LANES == 0
    q_blk = pl.BlockSpec((1, tq, LANES), lambda bi, h, i: (bi, i, h))
    return pl.pallas_call(
        functools.partial(_diff_attn_kernel, tk=tk, n_chunks=t // tk, n_ctx=n_ctx, lambda_init=lambda_init),
        out_shape=jax.ShapeDtypeStruct((b, t, D_A), BF16),
        grid=(b, H_A, t // tq),
        in_specs=[pl.BlockSpec((4, HD), lambda bi, h, i: (0, 0)),
                  pl.BlockSpec((1, 2 * HD), lambda bi, h, i: (0, 0)),
                  q_blk,
                  pl.BlockSpec((1, n_ctx, LANES), lambda bi, h, i: (bi, 0, h)),
                  pl.BlockSpec((1, LANES, n_ctx), lambda bi, h, i: (bi, h, 0)),
                  pl.BlockSpec((1, t, LANES), lambda bi, h, i: (bi, 0, h)),
                  pl.BlockSpec((1, LANES, t), lambda bi, h, i: (bi, h, 0))],
        out_specs=q_blk,
        scratch_shapes=[pltpu.VMEM((2, 2, tk, tq), F32),
                        pltpu.VMEM((2, 2, tk, tq), BF16),
                        pltpu.VMEM((2, 2 * HD + DENOM_ROWS, tq), F32)],
        compiler_params=_cparams(3), name="diff_attn",
    )(lp, g, qa, ka_c, vat_c, ka, vat)


NA_QROWS = 4
NA_GROUPS = 4
NA_KROWS = NA_QROWS + WIN_R
NA_REL_PAD = NA_QROWS
NA_REL_N = NA_KROWS + WIN_R + NA_QROWS - 2


def _na_kernel(q_ref, k_ref, vt_ref, kc_ref, vct_ref, tab_ref, o_ref, s_ref, p_ref, ot_ref, *, rows, n_ctx):
    n_loc = NA_KROWS * GRID_W
    nq = NA_QROWS * GRID_W
    lo, hi = _half_masks(BF16)
    qr = lax.broadcasted_iota(jnp.int32, (1, nq), 1) // GRID_W

    def geometry(gi):
        r0 = (pl.program_id(1) * NA_GROUPS + gi) * NA_QROWS
        ws = jnp.clip(r0 - WIN_R // 2, 0, rows - NA_KROWS)
        rs = jnp.clip(r0 + qr - WIN_R // 2, 0, rows - WIN_R)
        row_mask = [jnp.where((ws + kr >= rs) & (ws + kr < rs + WIN_R), 0.0, NEG_BIG) for kr in range(NA_KROWS)]
        return r0 - ws, pl.multiple_of(ws * GRID_W, NA_QROWS * GRID_W), row_mask

    geo = [geometry(gi) for gi in range(NA_GROUPS)]
    items = [(gi, h) for gi in range(NA_GROUPS) for h in range(H_B)]

    def scores(it):
        gi, h = items[it]
        d0, k_start, row_mask = geo[gi]
        hp, e = divmod(h, 2)
        cols = slice(hp * LANES, (hp + 1) * LANES)
        qm = q_ref[0, gi * nq:(gi + 1) * nq, cols] * (lo, hi)[e]
        s_loc = lax.dot_general(k_ref[0, pl.ds(k_start, n_loc), cols], qm, _NT, preferred_element_type=F32)
        for kr in range(NA_KROWS):
            base = kr - d0 + (WIN_R - 1) + (NA_REL_PAD - 1)
            bias = jnp.concatenate([tab_ref[h, base - 2 * p] for p in range(NA_QROWS // 2)], axis=1)
            rows_kr = slice(kr * GRID_W, (kr + 1) * GRID_W)
            s_ref[it % 2, rows_kr, :] = s_loc[rows_kr] + bias + row_mask[kr]
        s_ref[it % 2, n_loc:, :] = lax.dot_general(kc_ref[0, :, cols], qm, _NT, preferred_element_type=F32)

    def softmax(it):
        s = s_ref[it % 2]
        m = s.max(axis=0, keepdims=True)
        p_ref[it % 2] = jnp.exp2(s - m).astype(BF16)

    def values(it):
        gi, h = items[it]
        k_start = geo[gi][1]
        hrows = slice(h * HD, (h + 1) * HD)
        vt = jnp.concatenate([vt_ref[0, hrows, pl.ds(k_start, n_loc)], vct_ref[0, hrows, :]], axis=1)
        vext = jnp.concatenate([vt, jnp.ones((DENOM_ROWS, n_loc + n_ctx), BF16)], axis=0)
        o = jnp.dot(vext, p_ref[it % 2], preferred_element_type=F32)
        ot_ref[gi, hrows, :] = o[:HD] / o[HD:HD + 1]

    n_items = len(items)
    for i in range(-2, n_items):
        if i + 2 < n_items:
            scores(i + 2)
        if 0 <= i + 1 < n_items:
            softmax(i + 1)
        if i >= 0:
            values(i)
    for gi in range(NA_GROUPS):
        o_ref[0, gi * nq:(gi + 1) * nq, :] = ot_ref[gi].T.astype(BF16)


def _na_bias_table(rpb):
    c = jnp.arange(GRID_W)[None, :]
    kc = jnp.arange(GRID_W)[:, None]
    cstart = jnp.clip(c - WIN_C // 2, 0, GRID_W - WIN_C)
    inside = (kc >= cstart) & (kc < cstart + WIN_C)
    rel_c = jnp.clip(kc - c + (WIN_C - 1), 0, 2 * WIN_C - 2)
    band = jnp.where(inside, rpb[:, :, rel_c] * math.log2(math.e), NEG_BIG)
    hi_pad = NA_REL_N - NA_REL_PAD + 1 - (2 * WIN_R - 1)
    band = jnp.pad(band, ((0, 0), (NA_REL_PAD, hi_pad), (0, 0), (0, 0)), constant_values=NEG_BIG)
    return jnp.concatenate([band[:, 1:], band[:, :-1]], axis=-1).astype(F32)


def _na_attn(qb, kb, vbt, kb_c, vbt_c, tab):
    b, t, _ = qb.shape
    rows = t // GRID_W
    n_ctx = kb_c.shape[1]
    step_rows = NA_QROWS * NA_GROUPS
    assert rows % step_rows == 0 and rows >= NA_KROWS and tab.shape[1] == NA_REL_N
    nq = NA_QROWS * GRID_W
    n_keys = NA_KROWS * GRID_W + n_ctx
    qblk = pl.BlockSpec((1, step_rows * GRID_W, D_B), lambda bi, g: (bi, g, 0))
    return pl.pallas_call(
        functools.partial(_na_kernel, rows=rows, n_ctx=n_ctx),
        out_shape=jax.ShapeDtypeStruct((b, t, D_B), BF16),
        grid=(b, rows // step_rows),
        in_specs=[qblk,
                  pl.BlockSpec((1, t, D_B), lambda bi, g: (bi, 0, 0)),
                  pl.BlockSpec((1, D_B, t), lambda bi, g: (bi, 0, 0)),
                  pl.BlockSpec((1, n_ctx, D_B), lambda bi, g: (bi, 0, 0)),
                  pl.BlockSpec((1, D_B, n_ctx), lambda bi, g: (bi, 0, 0)),
                  pl.BlockSpec(tab.shape, lambda bi, g: (0, 0, 0, 0))],
        out_specs=qblk,
        scratch_shapes=[pltpu.VMEM((2, n_keys, nq), F32),
                        pltpu.VMEM((2, n_keys, nq), BF16),
                        pltpu.VMEM((NA_GROUPS, D_B, nq), F32)],
        compiler_params=_cparams(2), name="na_attn",
    )(qb, kb, vbt, kb_c, vbt_c, tab)


def _ctx_attn_kernel(lp_ref, g_ref, qa_ref, ka_ref, vat_ref, qb_ref, kb_ref, vbt_ref,
                     oa_ref, ob_ref, *, lambda_init):
    lam = _lambda_full(lp_ref, lambda_init)
    lo, hi = _half_masks(BF16)
    lane = lax.broadcasted_iota(jnp.int32, (1, LANES), 1)
    for h in range(H_A):
        cols = slice(h * LANES, (h + 1) * LANES)
        q, k, vt = qa_ref[0, :, cols], ka_ref[0, :, cols], vat_ref[0, cols, :]
        probs = []
        for msk in (lo, hi):
            s = lax.dot_general(q * msk, k, _NT, preferred_element_type=F32)
            (p,), inv = _softmax_parts([s])
            probs.append(p * inv)
        w = (probs[0] - lam * probs[1]).astype(BF16)
        o = lax.dot_general(w, vt, _NT, preferred_element_type=F32)
        oa_ref[0, :, cols] = _head_rms(o, g_ref[...], lambda_init).astype(BF16)
    for hp in range(H_B // 2):
        cols = slice(hp * LANES, (hp + 1) * LANES)
        q, k, vt = qb_ref[0, :, cols], kb_ref[0, :, cols], vbt_ref[0, cols, :]
        halves = []
        for msk in (lo, hi):
            s = lax.dot_general(q * msk, k, _NT, preferred_element_type=F32)
            (p,), inv = _softmax_parts([s])
            halves.append(lax.dot_general((p * inv).astype(BF16), vt, _NT, preferred_element_type=F32))
        ob_ref[0, :, cols] = jnp.where(lane < HD, halves[0], halves[1]).astype(BF16)


def _ctx_attn(lp, g, qa, ka, vat, qb, kb, vbt, lambda_init):
    b, n, _ = qa.shape
    blk = pl.BlockSpec((1, n, D_A), lambda bi: (bi, 0, 0))
    blk_t = pl.BlockSpec((1, D_A, n), lambda bi: (bi, 0, 0))
    out = jax.ShapeDtypeStruct((b, n, D_A), BF16)
    return pl.pallas_call(
        functools.partial(_ctx_attn_kernel, lambda_init=lambda_init),
        out_shape=(out, out),
        grid=(b,),
        in_specs=[pl.BlockSpec((4, HD), lambda bi: (0, 0)),
                  pl.BlockSpec((1, 2 * HD), lambda bi: (0, 0)),
                  blk, blk, blk_t, blk, blk, blk_t],
        out_specs=(blk, blk),
        compiler_params=_cparams(1), name="ctx_attn",
    )(lp, g, qa, ka, vat, qb, kb, vbt)


def _outproj_ln_kernel(oa_ref, ob_ref, w_ref, x_ref, gate_ref, g_ref, b_ref, o_ref):
    o = (jnp.dot(oa_ref[0], w_ref[:D_A], preferred_element_type=F32)
         + jnp.dot(ob_ref[0], w_ref[D_A:], preferred_element_type=F32))
    y = ALPHA * x_ref[0] + gate_ref[0] * o
    o_ref[0] = _layer_norm(y, g_ref[...], b_ref[...])


def _outproj_ln(oa, ob, w, x, gate, ln_g, ln_b, tm):
    b, t, _ = x.shape
    half = pl.BlockSpec((1, tm, D_A), lambda bi, i: (bi, i, 0))
    full = pl.BlockSpec((1, tm, D_MODEL), lambda bi, i: (bi, i, 0))
    vec = pl.BlockSpec((1, 1, D_MODEL), lambda bi, i: (bi, 0, 0))
    par = pl.BlockSpec((1, D_MODEL), lambda bi, i: (0, 0))
    return pl.pallas_call(
        _outproj_ln_kernel,
        out_shape=jax.ShapeDtypeStruct((b, t, D_MODEL), F32),
        grid=(b, t // tm),
        in_specs=[half, half, pl.BlockSpec((D_MODEL, D_MODEL), lambda bi, i: (0, 0)),
                  full, vec, par, par],
        out_specs=full,
        compiler_params=_cparams(2), name="outproj_ln",
    )(oa, ob, w, x, gate, ln_g, ln_b)


def _fill_modulated(h_ref, x_ref, xp_ref, xn_ref, sc_ref, sh_ref, tm, reverse=False):
    i = pl.program_id(1)
    last = pl.num_programs(1) - 1
    ti = (last - i) if reverse else i
    scale = 1.0 + sc_ref[0]
    shift = sh_ref[0]
    h_ref[HALO:HALO + tm] = (x_ref[0] * scale + shift).astype(BF16)
    prev = xp_ref[0] * scale + shift
    nxt = xn_ref[0] * scale + shift
    h_ref[0:HALO] = jnp.where(ti > 0, prev, 0.0).astype(BF16)
    h_ref[HALO + tm:] = jnp.where(ti < last, nxt, 0.0).astype(BF16)


def _conv_ffn_kernel(x_ref, xp_ref, xn_ref, sc_ref, sh_ref, gate_ref, wu_ref, cw_ref, cb_ref, wd_ref,
                     lg_ref, lb_ref, o_ref, h_ref, ug_ref, uv_ref, a_ref, *, tm):
    _fill_modulated(h_ref, x_ref, xp_ref, xn_ref, sc_ref, sh_ref, tm)

    def conv3(u_ref, slot, wcols):
        u = u_ref[slot]
        cw = cw_ref[:, :, wcols]
        cb = cb_ref[:, wcols]
        rows = u.shape[0]
        u3 = u.reshape(rows // SUBLANES, SUBLANES, FF_CHUNK)
        sub = lax.broadcasted_iota(jnp.int32, u3.shape, 1)
        dn = pltpu.roll(u3, 1, 1)
        dn = jnp.where(sub == 0, jnp.concatenate([dn[-1:], dn[:-1]], axis=0), dn)
        up_ = pltpu.roll(u3, SUBLANES - 1, 1)
        up_ = jnp.where(sub == SUBLANES - 1, jnp.concatenate([up_[1:], up_[:1]], axis=0), up_)
        full = cw[0][None] * dn + cw[1][None] * u3 + cw[2][None] * up_ + cb[None]
        return full.reshape(rows, FF_CHUNK)[HALO:HALO + tm]

    def gate_cols(c):
        return slice(c * FF_CHUNK, (c + 1) * FF_CHUNK)

    def value_cols(c):
        return slice(D_FF + c * FF_CHUNK, D_FF + (c + 1) * FF_CHUNK)

    def up(c):
        h = h_ref[...]
        ug_ref[c % 2] = jnp.dot(h, wu_ref[:, gate_cols(c)], preferred_element_type=F32)
        uv_ref[c % 2] = jnp.dot(h, wu_ref[:, value_cols(c)], preferred_element_type=F32)

    def act(c):
        g = conv3(ug_ref, c % 2, gate_cols(c))
        v = conv3(uv_ref, c % 2, value_cols(c))
        t = jnp.tanh(g * (GELU_C1 + GELU_C2 * (g * g)))
        a_ref[:, gate_cols(c)] = ((g + g * t) * v).astype(BF16)

    up(0)
    for c in range(1, N_FF_CHUNKS):
        up(c)
        act(c - 1)
    act(N_FF_CHUNKS - 1)
    o = jnp.dot(a_ref[...], wd_ref[...], preferred_element_type=F32)
    y = ALPHA * x_ref[0] + gate_ref[0] * o
    o_ref[0] = _layer_norm(y, lg_ref[...], lb_ref[...])


def _halo_specs(t, tm):
    nb = t // HALO
    per = tm // HALO
    main = pl.BlockSpec((1, tm, D_MODEL), lambda bi, i: (bi, i, 0))
    prev = pl.BlockSpec((1, HALO, D_MODEL), lambda bi, i: (bi, jnp.maximum(i * per - 1, 0), 0))
    nxt = pl.BlockSpec((1, HALO, D_MODEL), lambda bi, i: (bi, jnp.minimum((i + 1) * per, nb - 1), 0))
    return main, prev, nxt


def _const_spec(shape):
    nd = len(shape)
    return pl.BlockSpec(shape, lambda bi, i: (0,) * nd, pipeline_mode=pl.Buffered(1))


def _conv_ffn_ln(x, sc, sh, gate, fw, ln_g, ln_b, tm):
    b, t, _ = x.shape
    main, prev, nxt = _halo_specs(t, tm)
    vec = pl.BlockSpec((1, 1, D_MODEL), lambda bi, i: (bi, 0, 0))
    par = _const_spec((1, D_MODEL))
    wu, cw, cb, wd = fw
    return pl.pallas_call(
        functools.partial(_conv_ffn_kernel, tm=tm),
        out_shape=jax.ShapeDtypeStruct((b, t, D_MODEL), F32),
        grid=(b, t // tm),
        in_specs=[main, prev, nxt, vec, vec, vec,
                  _const_spec(wu.shape), _const_spec(cw.shape), _const_spec(cb.shape),
                  _const_spec(wd.shape), par, par],
        out_specs=main,
        scratch_shapes=[pltpu.VMEM((tm + 2 * HALO, D_MODEL), BF16),
                        pltpu.VMEM((2, tm + 2 * HALO, FF_CHUNK), F32),
                        pltpu.VMEM((2, tm + 2 * HALO, FF_CHUNK), F32),
                        pltpu.VMEM((tm, D_FF), BF16)],
        compiler_params=_cparams(2), name="conv_ffn_ln",
    )(x, x, x, sc, sh, gate, wu, cw, cb, wd, ln_g, ln_b)


def _prep_ffn(w_up, conv_w, conv_b, w_down):
    half_v = jnp.concatenate([jnp.ones((D_FF,), F32), jnp.full((D_FF,), 0.5, F32)])
    cw = jnp.broadcast_to((conv_w * half_v)[:, None, :], (conv_w.shape[0], SUBLANES, 2 * D_FF))
    cb = jnp.broadcast_to((conv_b * half_v)[None, :], (SUBLANES, 2 * D_FF))
    return w_up.astype(BF16), cw, cb, w_down.astype(BF16)


def _rg_coeffs(xb, wgate_ref, bgate_ref, ap_ref, a_ref, b_ref, reverse):
    ap = ap_ref[...]
    neg = -ap
    sp = jnp.maximum(neg, 0.0) + jnp.log1p(jnp.exp(-jnp.abs(neg)))
    half_c = (-0.5 * RG_C) * sp
    for n in range(N_RG_BLOCKS):
        cols = slice(n * RG_BW, (n + 1) * RG_BW)
        xn = xb[:, cols]
        t = jnp.tanh(jnp.dot(xn.astype(BF16), wgate_ref[n], preferred_element_type=F32) + bgate_ref[n])
        log_a = half_c[:, cols] + half_c[:, cols] * t[:, :RG_BW]
        gate_x = 0.5 + 0.5 * t[:, RG_BW:]
        a = jnp.exp(log_a)
        one_minus_a2 = jnp.tanh(log_a) * (-1.0 - a * a)
        root = jnp.where(one_minus_a2 > 0.0, one_minus_a2 * lax.rsqrt(one_minus_a2), 0.0)
        a_ref[:, cols], b_ref[:, cols] = _group_prefix(a, gate_x * xn * root, reverse)


def _group_prefix(a, b, reverse):
    rows, cols = a.shape
    a = a.reshape(rows // SUBLANES, SUBLANES, cols)
    b = b.reshape(rows // SUBLANES, SUBLANES, cols)
    sub = lax.broadcasted_iota(jnp.int32, a.shape, 1)
    for k in (1, 2, 4):
        shift = (SUBLANES - k) if reverse else k
        msk = (sub < SUBLANES - k) if reverse else (sub >= k)
        a_sh = pltpu.roll(a, shift, 1)
        b_sh = pltpu.roll(b, shift, 1)
        b = jnp.where(msk, a * b_sh + b, b)
        a = jnp.where(msk, a * a_sh, a)
    return a.reshape(rows, cols), b.reshape(rows, cols)


def _scan_tile(a_ref, b_ref, r_ref, h_in, tt, reverse):
    n_groups = tt // SUBLANES
    h = h_in
    for gi in range(n_groups):
        g = (n_groups - 1 - gi) if reverse else gi
        rows = slice(g * SUBLANES, (g + 1) * SUBLANES)
        hg = a_ref[rows, :] * h + b_ref[rows, :]
        r_ref[rows, :] = hg
        h = hg[0:1] if reverse else hg[SUBLANES - 1:SUBLANES]
    return h


def _rnn_in_kernel(x_ref, xp_ref, xn_ref, sc_ref, sh_ref, w_ref, cw_ref, cb_ref,
                   wgate_ref, bgate_ref, ap_ref, h0_ref,
                   r_ref, y_ref, xb_ref, h_ref, a_ref, b_ref, carry_ref, *, tt, reverse):
    i = pl.program_id(1)

    @pl.when(i == 0)
    def _():
        carry_ref[...] = h0_ref[0]

    _fill_modulated(h_ref, x_ref, xp_ref, xn_ref, sc_ref, sh_ref, tt, reverse)
    h = h_ref[...]
    y_ref[0] = jnp.dot(h[HALO:HALO + tt], w_ref[:, :D_MODEL], preferred_element_type=F32).astype(BF16)
    u = jnp.dot(h, w_ref[:, D_MODEL:], preferred_element_type=F32)
    u3 = u.reshape((tt + 2 * HALO) // SUBLANES, SUBLANES, D_MODEL)
    sub = lax.broadcasted_iota(jnp.int32, u3.shape, 1)

    def shifted(k):
        rolled = pltpu.roll(u3, k % SUBLANES, 1)
        if k > 0:
            return jnp.where(sub < k, jnp.concatenate([rolled[-1:], rolled[:-1]], axis=0), rolled)
        return jnp.where(sub >= SUBLANES + k, jnp.concatenate([rolled[1:], rolled[:1]], axis=0), rolled)

    full = (cw_ref[0][None] * shifted(2) + cw_ref[1][None] * shifted(1) + cw_ref[2][None] * u3
            + cw_ref[3][None] * shifted(-1) + cb_ref[...][None])
    xb = full.reshape(tt + 2 * HALO, D_MODEL)[HALO:HALO + tt]
    xb_ref[0] = xb
    _rg_coeffs(xb, wgate_ref, bgate_ref, ap_ref, a_ref, b_ref, reverse)
    carry_ref[...] = _scan_tile(a_ref, b_ref, r_ref.at[0], carry_ref[...], tt, reverse)


def _rnn_in(x, sc, sh, w_in, conv_w, conv_b, wgate, bgate, ap, h0, tt, reverse):
    b, t, _ = x.shape
    nt = t // tt
    nb = t // HALO
    per = tt // HALO

    def tix(i):
        return (nt - 1 - i) if reverse else i

    main = pl.BlockSpec((1, tt, D_MODEL), lambda bi, i: (bi, tix(i), 0))
    prev = pl.BlockSpec((1, HALO, D_MODEL), lambda bi, i: (bi, jnp.maximum(tix(i) * per - 1, 0), 0))
    nxt = pl.BlockSpec((1, HALO, D_MODEL), lambda bi, i: (bi, jnp.minimum((tix(i) + 1) * per, nb - 1), 0))
    vec = pl.BlockSpec((1, 1, D_MODEL), lambda bi, i: (bi, 0, 0))
    return pl.pallas_call(
        functools.partial(_rnn_in_kernel, tt=tt, reverse=reverse),
        out_shape=(jax.ShapeDtypeStruct((b, t, D_MODEL), F32),
                   jax.ShapeDtypeStruct((b, t, D_MODEL), BF16),
                   jax.ShapeDtypeStruct((b, t, D_MODEL), F32)),
        grid=(b, nt),
        in_specs=[main, prev, nxt, vec, vec,
                  _const_spec(w_in.shape), _const_spec(conv_w.shape), _const_spec(conv_b.shape),
                  _const_spec(wgate.shape), _const_spec(bgate.shape), _const_spec((1, D_MODEL)), vec],
        out_specs=(main, main, main),
        scratch_shapes=[pltpu.VMEM((tt + 2 * HALO, D_MODEL), BF16),
                        pltpu.VMEM((tt, D_MODEL), F32),
                        pltpu.VMEM((tt, D_MODEL), F32),
                        pltpu.VMEM((1, D_MODEL), F32)],
        compiler_params=_cparams(2), name="rnn_in_rev" if reverse else "rnn_in_fwd",
    )(x, x, x, sc, sh, w_in, conv_w, conv_b, wgate, bgate, ap, h0)


def _rnn_out_kernel(xb_ref, y_ref, rf_ref, x_ref, gate_ref, wgate_ref, bgate_ref, ap_ref, h0_ref,
                    w_ref, lg_ref, lb_ref, o_ref, a_ref, b_ref, rb_ref, carry_ref, *, tt):
    i = pl.program_id(1)

    @pl.when(i == 0)
    def _():
        carry_ref[...] = h0_ref[0]

    _rg_coeffs(xb_ref[0], wgate_ref, bgate_ref, ap_ref, a_ref, b_ref, True)
    carry_ref[...] = _scan_tile(a_ref, b_ref, rb_ref, carry_ref[...], tt, True)
    r = rf_ref[0] + rb_ref[...]
    yg = y_ref[0].astype(F32)
    hy = 0.5 * yg
    m = (r * (hy + hy * jnp.tanh(yg * (GELU_C1 + GELU_C2 * (yg * yg))))).astype(BF16)
    o = jnp.dot(m, w_ref[...], preferred_element_type=F32)
    y = ALPHA * x_ref[0] + gate_ref[0] * o
    o_ref[0] = _layer_norm(y, lg_ref[...], lb_ref[...])


def _rnn_out(xb, y, rf, x, gate, wgate, bgate, ap, h0, w_out, ln_g, ln_b, tt):
    b, t, _ = x.shape
    nt = t // tt
    main = pl.BlockSpec((1, tt, D_MODEL), lambda bi, i: (bi, nt - 1 - i, 0))
    vec = pl.BlockSpec((1, 1, D_MODEL), lambda bi, i: (bi, 0, 0))
    par = _const_spec((1, D_MODEL))
    return pl.pallas_call(
        functools.partial(_rnn_out_kernel, tt=tt),
        out_shape=jax.ShapeDtypeStruct((b, t, D_MODEL), F32),
        grid=(b, nt),
        in_specs=[main, main, main, main, vec,
                  _const_spec(wgate.shape), _const_spec(bgate.shape), par, vec,
                  _const_spec(w_out.shape), par, par],
        out_specs=main,
        scratch_shapes=[pltpu.VMEM((tt, D_MODEL), F32)] * 3 + [pltpu.VMEM((1, D_MODEL), F32)],
        compiler_params=_cparams(2), name="rnn_out",
    )(xb, y, rf, x, gate, wgate, bgate, ap, h0, w_out, ln_g, ln_b)


def _rope_tables(n_tok):
    t = jnp.arange(n_tok)
    row = (t // GRID_W).astype(F32)[:, None]
    col = (t % GRID_W).astype(F32)[:, None]
    nf = HD // 4
    inv = 1.0 / (ROPE_THETA ** (jnp.arange(nf, dtype=F32) / nf))
    ang = jnp.concatenate([row * inv, row * inv, col * inv, col * inv], -1)
    sign = jnp.where((jnp.arange(HD) % (HD // 2)) < nf, -1.0, 1.0)
    reps = D_A // HD
    return jnp.tile(jnp.cos(ang), (1, reps)), jnp.tile(jnp.sin(ang) * sign, (1, reps))


ROW_TILE = 512
FFN_ROW_TILE = 1024
ATTN_Q_TILE = 1024
ATTN_K_CHUNK = 512


def _tile(n, pref):
    return pref if n % pref == 0 else n


def kernel(x, c, ctx, c_ctx, ada_w, ada_b, ln1_g, ln1_b, ln2_g, ln2_b, ffn_w_up, ffn_conv_w, ffn_conv_b, ffn_w_down, att_w_in, att_w_out, diff_lq1, diff_lk1, diff_lq2, diff_lk2, diff_subln_g, na_rpb, rnn_w_in, rnn_conv_w, rnn_conv_b, rg_a_param, rg_wa, rg_ba, rg_wx, rg_bx, rnn_w_out):
    bsz, t, _ = x.shape
    n_ctx = ctx.shape[1]
    assert t % GRID_W == 0 and t // GRID_W >= WIN_R and n_ctx % HALO == 0

    n_rows = -(-(bsz + 1) // SUBLANES) * SUBLANES
    cc = jnp.zeros((n_rows, D_MODEL), F32).at[:bsz].set(c).at[bsz].set(c_ctx)
    mod = _ada_mod(cc, ada_w, ada_b)

    def mods(l):
        lat = mod[l, :bsz].reshape(bsz, 1, 6, D_MODEL)
        cx = jnp.broadcast_to(mod[l, bsz].reshape(1, 1, 6, D_MODEL), (bsz, 1, 6, D_MODEL))
        return ([lat[:, :, k] for k in range(6)], [cx[:, :, k] for k in range(6)])

    tm = _tile(t, ROW_TILE)
    tmc = _tile(n_ctx, ROW_TILE)

    (sh1, sc1, g1, sh2, sc2, g2), (csh1, csc1, cg1, csh2, csc2, cg2) = mods(0)
    lambda_init = 0.8 - 0.6 * math.exp(-0.3 * 0)
    w_in = att_w_in[0].astype(BF16)
    w_out = att_w_out[0].astype(BF16)
    cos_t, sin_t = _rope_tables(t)
    qa, ka, vat, qb, kb, vbt = _proj_attn(x, sc1, sh1, w_in, cos_t, sin_t, True, tm)
    qa_c, ka_c, vat_c, qb_c, kb_c, vbt_c = _proj_attn(
        ctx, csc1, csh1, w_in, cos_t[:n_ctx], sin_t[:n_ctx], False, tmc)
    lp = jnp.stack([diff_lq1[0], diff_lk1[0], diff_lq2[0], diff_lk2[0]])
    sub_g = diff_subln_g[0].reshape(1, 2 * HD)
    oa = _diff_attn(lp, sub_g, qa, ka_c, vat_c, ka, vat, lambda_init,
                    _tile(t, ATTN_Q_TILE), _tile(t, ATTN_K_CHUNK))
    ob = _na_attn(qb, kb, vbt, kb_c, vbt_c, _na_bias_table(na_rpb[0]))
    oa_c, ob_c = _ctx_attn(lp, sub_g, qa_c, ka_c, vat_c, qb_c, kb_c, vbt_c, lambda_init)
    l1g, l1b = ln1_g[0].reshape(1, D_MODEL), ln1_b[0].reshape(1, D_MODEL)
    l2g, l2b = ln2_g[0].reshape(1, D_MODEL), ln2_b[0].reshape(1, D_MODEL)
    x = _outproj_ln(oa, ob, w_out, x, g1, l1g, l1b, tm)
    ctx = _outproj_ln(oa_c, ob_c, w_out, ctx, cg1, l1g, l1b, tmc)
    fw = _prep_ffn(ffn_w_up[0], ffn_conv_w[0], ffn_conv_b[0], ffn_w_down[0])
    x = _conv_ffn_ln(x, sc2, sh2, g2, fw, l2g, l2b, _tile(t, FFN_ROW_TILE))
    ctx = _conv_ffn_ln(ctx, csc2, csh2, cg2, fw, l2g, l2b, tmc)

    (sh1, sc1, g1, sh2, sc2, g2), (csh1, csc1, _, _, _, _) = mods(1)
    w_in = rnn_w_in[0].astype(BF16)
    w_out = rnn_w_out[0].astype(BF16)
    conv_w = jnp.broadcast_to(rnn_conv_w[0][:, None, :], (CONV_RNN, SUBLANES, D_MODEL))
    conv_b = jnp.broadcast_to(rnn_conv_b[0][None, :], (SUBLANES, D_MODEL))
    wgate = [(0.5 * jnp.concatenate([rg_wa[0, d], rg_wx[0, d]], axis=-1)).astype(BF16) for d in range(2)]
    bgate = [0.5 * jnp.concatenate([rg_ba[0, d], rg_bx[0, d]], axis=-1).reshape(N_RG_BLOCKS, 1, 2 * RG_BW)
             for d in range(2)]
    ap = [rg_a_param[0, d].reshape(1, D_MODEL) for d in range(2)]
    zeros = jnp.zeros((bsz, 1, D_MODEL), F32)
    rc_f, _, _ = _rnn_in(ctx, csc1, csh1, w_in, conv_w, conv_b, wgate[0], bgate[0], ap[0],
                         zeros, tmc, False)
    rc_b, _, _ = _rnn_in(ctx, csc1, csh1, w_in, conv_w, conv_b, wgate[1], bgate[1], ap[1],
                         zeros, tmc, True)
    h0_f = rc_f[:, n_ctx - 1:n_ctx]
    h0_b = rc_b[:, 0:1]
    r_f, y, xb = _rnn_in(x, sc1, sh1, w_in, conv_w, conv_b, wgate[0], bgate[0], ap[0],
                         h0_f, tm, False)
    l1g, l1b = ln1_g[1].reshape(1, D_MODEL), ln1_b[1].reshape(1, D_MODEL)
    l2g, l2b = ln2_g[1].reshape(1, D_MODEL), ln2_b[1].reshape(1, D_MODEL)
    x = _rnn_out(xb, y, r_f, x, g1, wgate[1], bgate[1], ap[1], h0_b, w_out, l1g, l1b, tm)
    fw = _prep_ffn(ffn_w_up[1], ffn_conv_w[1], ffn_conv_b[1], ffn_w_down[1])
    x = _conv_ffn_ln(x, sc2, sh2, g2, fw, l2g, l2b, _tile(t, FFN_ROW_TILE))
    return x
```
